```python
import math
import jax
import jax.numpy as jnp
from jax import lax
import numpy as np

D_MODEL = 1024
BATCH = 8
SEQ = 2048
DEPTH = 4

CTX_LEN = 256
GRID_W = 64
HEAD_DIM = 64
MIX_W = D_MODEL
A_HEADS = D_MODEL // (4 * HEAD_DIM)
A_DK = HEAD_DIM // 2
A_DV = HEAD_DIM
A_W = A_HEADS * HEAD_DIM
B_HEADS = 3 * D_MODEL // (8 * HEAD_DIM)
B_KV_HEADS = 2
B_W = B_HEADS * HEAD_DIM
B_KV_W = B_KV_HEADS * HEAD_DIM
C_HEADS = 3 * D_MODEL // (8 * HEAD_DIM)
C_W = C_HEADS * HEAD_DIM
WIN_H = 8
WIN_W = 16
Q_BLOCK = 128
ROPE_THETA = 10000.0
EPS = 1e-6
SPLIT_SIZES = (A_W, A_W, A_W, A_W, B_W, B_KV_W, B_KV_W, B_W, C_W, C_W, C_W, C_W)
IN_W = sum(SPLIT_SIZES)

kernel_name = 'hybrid_diff_gqa_natten_prefix_dit'


def rms_norm(x, g):
    xf = x.astype(jnp.float32)
    y = xf * lax.rsqrt(jnp.mean(xf * xf, axis=-1, keepdims=True) + EPS)
    return (y * g.astype(jnp.float32)).astype(x.dtype)


def axial_rope(x, row, col):
    d = x.shape[-1]
    nf = d // 4
    inv = ROPE_THETA ** (-jnp.arange(nf, dtype=jnp.float32) / nf)
    ang = jnp.concatenate([row[:, None] * inv, col[:, None] * inv], axis=-1)
    shape = (1, x.shape[1]) + (1,) * (x.ndim - 3) + (d // 2,)
    cos = jnp.cos(ang).reshape(shape)
    sin = jnp.sin(ang).reshape(shape)
    xf = x.astype(jnp.float32)
    x1, x2 = xf[..., : d // 2], xf[..., d // 2:]
    return jnp.concatenate([x1 * cos - x2 * sin, x2 * cos + x1 * sin], axis=-1).astype(x.dtype)


def map_query_blocks(fn, q):
    b, s = q.shape[:2]
    nb = s // Q_BLOCK
    qb = jnp.moveaxis(q.reshape((b, nb, Q_BLOCK) + q.shape[2:]), 1, 0)
    out = lax.map(fn, qb)
    return jnp.moveaxis(out, 0, 1).reshape((b, s) + out.shape[3:])


def diff_attend(q, k, v, lam):
    s = jnp.einsum('bqhmd,bkhmd->bhmqk', q, k).astype(jnp.float32) * (q.shape[-1] ** -0.5)
    p = jax.nn.softmax(s, axis=-1)
    a = (p[:, :, 0] - lam * p[:, :, 1]).astype(v.dtype)
    return jnp.einsum('bhqk,bkhd->bqhd', a, v)


def gqa_attend(q, k, v):
    b, nq, h, d = q.shape
    kvh = k.shape[2]
    qg = q.reshape(b, nq, kvh, h // kvh, d)
    s = jnp.einsum('bqhgd,bkhd->bhgqk', qg, k).astype(jnp.float32) * (d ** -0.5)
    p = jax.nn.softmax(s, axis=-1).astype(v.dtype)
    o = jnp.einsum('bhgqk,bkhd->bqhgd', p, v)
    return o.reshape(b, nq, h * v.shape[-1])


def neighbourhood_attend(q, k, v, k_ctx, v_ctx, rpb, rows):
    b, s, h, d = q.shape
    kh = min(WIN_H, rows)
    kw = WIN_W
    scale = d ** -0.5
    qg = q.reshape(b, rows, GRID_W, h, d)
    kg = k.reshape(b, rows, GRID_W, h, d)
    vg = v.reshape(b, rows, GRID_W, h, d)
    r = jnp.arange(rows)
    cidx = jnp.arange(GRID_W)
    r0 = jnp.clip(r - kh // 2, 0, rows - kh)
    key_rows = r0[:, None] + jnp.arange(kh)
    k_blk = kg[:, key_rows]
    v_blk = vg[:, key_rows]
    c0 = jnp.clip(cidx - kw // 2, 0, GRID_W - kw)
    in_win = (cidx[None, :] >= c0[:, None]) & (cidx[None, :] < c0[:, None] + kw)
    dr = key_rows - r[:, None] + (WIN_H - 1)
    dc = jnp.clip(cidx[None, :] - cidx[:, None] + (WIN_W - 1), 0, 2 * WIN_W - 2)
    bias = rpb[:, dr[:, None, :, None], dc[None, :, None, :]]
    s_nb = jnp.einsum('brqhd,brkwhd->bhrqkw', qg, k_blk).astype(jnp.float32) * scale
    s_nb = s_nb + bias.astype(jnp.float32)[None]
    s_nb = jnp.where(in_win[:, None, :], s_nb, -jnp.inf)
    s_cx = jnp.einsum('brqhd,bchd->bhrqc', qg, k_ctx).astype(jnp.float32) * scale
    n_nb = kh * GRID_W
    sc = jnp.concatenate([s_nb.reshape(b, h, rows, GRID_W, n_nb), s_cx], axis=-1)
    p = jax.nn.softmax(sc, axis=-1).astype(v.dtype)
    p_nb = p[..., :n_nb].reshape(b, h, rows, GRID_W, kh, GRID_W)
    p_cx = p[..., n_nb:]
    o = jnp.einsum('bhrqkw,brkwhd->brqhd', p_nb, v_blk) + jnp.einsum('bhrqc,bchd->brqhd', p_cx, v_ctx)
    return o.reshape(b, s, h * d)


def split_proj(p):
    idx = np.cumsum(SPLIT_SIZES)[:-1].tolist()
    return jnp.split(p, idx, axis=-1)


def setup_inputs(seed: int = 0) -> dict:
    key = jax.random.key(seed)
    ks = jax.random.split(key, 22)
    nrm = jax.random.normal
    f32 = jnp.float32
    return {
        'x': nrm(ks[0], (BATCH, SEQ, D_MODEL), f32),
        'c': nrm(ks[1], (BATCH, D_MODEL), f32),
        'ctx': nrm(ks[2], (BATCH, CTX_LEN, D_MODEL), f32),
        'c_ctx': nrm(ks[3], (D_MODEL,), f32),
        'norm_g': 1.0 + 0.05 * nrm(ks[4], (DEPTH, D_MODEL), f32),
        'w_ada': nrm(ks[5], (DEPTH, D_MODEL, 3 * D_MODEL), f32) * (0.5 * D_MODEL ** -0.5),
        'b_ada': 0.01 * nrm(ks[6], (DEPTH, 3 * D_MODEL), f32),
        'w_in': nrm(ks[7], (DEPTH, D_MODEL, IN_W), f32) * (D_MODEL ** -0.5),
        'w_out': nrm(ks[8], (DEPTH, MIX_W, D_MODEL), f32) * (MIX_W ** -0.5),
        'diff_q_norm': 1.0 + 0.05 * nrm(ks[9], (DEPTH, A_DK), f32),
        'diff_k_norm': 1.0 + 0.05 * nrm(ks[10], (DEPTH, A_DK), f32),
        'lambda_q1': 0.1 * nrm(ks[11], (DEPTH, A_DK), f32),
        'lambda_k1': 0.1 * nrm(ks[12], (DEPTH, A_DK), f32),
        'lambda_q2': 0.1 * nrm(ks[13], (DEPTH, A_DK), f32),
        'lambda_k2': 0.1 * nrm(ks[14], (DEPTH, A_DK), f32),
        'diff_subln': 1.0 + 0.05 * nrm(ks[15], (DEPTH, A_DV), f32),
        'gqa_q_norm': 1.0 + 0.05 * nrm(ks[16], (DEPTH, HEAD_DIM), f32),
        'gqa_k_norm': 1.0 + 0.05 * nrm(ks[17], (DEPTH, HEAD_DIM), f32),
        'nat_q_norm': 1.0 + 0.05 * nrm(ks[18], (DEPTH, HEAD_DIM), f32),
        'nat_k_norm': 1.0 + 0.05 * nrm(ks[19], (DEPTH, HEAD_DIM), f32),
        'nat_rpb': 0.1 * nrm(ks[20], (DEPTH, C_HEADS, 2 * WIN_H - 1, 2 * WIN_W - 1), f32),
    }


def reference(x, c, ctx, c_ctx, norm_g, w_ada, b_ada, w_in, w_out, diff_q_norm, diff_k_norm,
              lambda_q1, lambda_k1, lambda_q2, lambda_k2, diff_subln, gqa_q_norm, gqa_k_norm,
              nat_q_norm, nat_k_norm, nat_rpb):
    b, s, _ = x.shape
    n_ctx = ctx.shape[1]
    rows = s // GRID_W
    t = jnp.arange(s)
    row = (t // GRID_W).astype(jnp.float32)
    col = (t % GRID_W).astype(jnp.float32)
    for l in range(DEPTH):
        lam_init = 0.8 - 0.6 * math.exp(-0.3 * l)
        sh, sc, gt = jnp.split(jax.nn.silu(c) @ w_ada[l] + b_ada[l], 3, axis=-1)
        sh_c, sc_c, gt_c = jnp.split(jax.nn.silu(c_ctx) @ w_ada[l] + b_ada[l], 3, axis=-1)
        h = rms_norm(x, norm_g[l]) * (1 + sc[:, None]) + sh[:, None]
        hc = rms_norm(ctx, norm_g[l]) * (1 + sc_c) + sh_c
        qa, ka, va, ga, qb, kb, vb, gb, qn, kn, vn, gn = split_proj(h @ w_in[l])
        qa_c, ka_c, va_c, ga_c, qb_c, kb_c, vb_c, gb_c, qn_c, kn_c, vn_c, gn_c = split_proj(hc @ w_in[l])

        lam = (jnp.exp(jnp.sum(lambda_q1[l] * lambda_k1[l])) - jnp.exp(jnp.sum(lambda_q2[l] * lambda_k2[l]))
               + lam_init).astype(jnp.float32)
        qa = axial_rope(rms_norm(qa.reshape(b, s, A_HEADS, 2, A_DK), diff_q_norm[l]), row, col)
        ka = axial_rope(rms_norm(ka.reshape(b, s, A_HEADS, 2, A_DK), diff_k_norm[l]), row, col)
        ka_c = rms_norm(ka_c.reshape(b, n_ctx, A_HEADS, 2, A_DK), diff_k_norm[l])
        va_c = va_c.reshape(b, n_ctx, A_HEADS, A_DV)
        ka_all = jnp.concatenate([ka, ka_c], axis=1)
        va_all = jnp.concatenate([va.reshape(b, s, A_HEADS, A_DV), va_c], axis=1)
        oa = map_query_blocks(lambda q_blk: diff_attend(q_blk, ka_all, va_all, lam), qa)
        ya = (rms_norm(oa, diff_subln[l]) * (1 - lam_init)).reshape(b, s, A_W) * jax.nn.silu(ga)

        qb = axial_rope(rms_norm(qb.reshape(b, s, B_HEADS, HEAD_DIM), gqa_q_norm[l]), row, col)
        kb = axial_rope(rms_norm(kb.reshape(b, s, B_KV_HEADS, HEAD_DIM), gqa_k_norm[l]), row, col)
        kb_c = rms_norm(kb_c.reshape(b, n_ctx, B_KV_HEADS, HEAD_DIM), gqa_k_norm[l])
        vb_c = vb_c.reshape(b, n_ctx, B_KV_HEADS, HEAD_DIM)
        kb_all = jnp.concatenate([kb, kb_c], axis=1)
        vb_all = jnp.concatenate([vb.reshape(b, s, B_KV_HEADS, HEAD_DIM), vb_c], axis=1)
        yb = map_query_blocks(lambda q_blk: gqa_attend(q_blk, kb_all, vb_all), qb) * jax.nn.silu(gb)

        qn = rms_norm(qn.reshape(b, s, C_HEADS, HEAD_DIM), nat_q_norm[l])
        kn = rms_norm(kn.reshape(b, s, C_HEADS, HEAD_DIM), nat_k_norm[l])
        kn_c = rms_norm(kn_c.reshape(b, n_ctx, C_HEADS, HEAD_DIM), nat_k_norm[l])
        vn_c = vn_c.reshape(b, n_ctx, C_HEADS, HEAD_DIM)
        yn = neighbourhood_attend(qn, kn, vn.reshape(b, s, C_HEADS, HEAD_DIM), kn_c, vn_c, nat_rpb[l], rows)
        yn = yn * jax.nn.silu(gn)

        y = jnp.concatenate([ya, yb, yn], axis=-1) @ w_out[l]

        if l < DEPTH - 1:
            qa_c = rms_norm(qa_c.reshape(b, n_ctx, A_HEADS, 2, A_DK), diff_q_norm[l])
            oa_c = diff_attend(qa_c, ka_c, va_c, lam)
            ya_c = (rms_norm(oa_c, diff_subln[l]) * (1 - lam_init)).reshape(b, n_ctx, A_W) * jax.nn.silu(ga_c)
            qb_c = rms_norm(qb_c.reshape(b, n_ctx, B_HEADS, HEAD_DIM), gqa_q_norm[l])
            yb_c = gqa_attend(qb_c, kb_c, vb_c) * jax.nn.silu(gb_c)
            qn_c = rms_norm(qn_c.reshape(b, n_ctx, C_HEADS, HEAD_DIM), nat_q_norm[l])
            yn_c = gqa_attend(qn_c, kn_c, vn_c) * jax.nn.silu(gn_c)
            ctx = ctx + gt_c * (jnp.concatenate([ya_c, yb_c, yn_c], axis=-1) @ w_out[l])

        x = x + gt[:, None] * y
    return x
```

```python
import functools
import math

import numpy as np
import jax
import jax.numpy as jnp
from jax import lax
from jax.experimental import pallas as pl
from jax.experimental.pallas import tpu as pltpu

D_MODEL = 1024
DEPTH = 4
GRID_W = 64
HEAD_DIM = 64
A_HEADS = 4
A_DK = 32
B_HEADS = 6
B_KV_HEADS = 2
C_HEADS = 6
WIN_H = 8
WIN_W = 16
ROPE_THETA = 10000.0
EPS = 1e-6
IN_W = 3584

LANES = 128
MXU_W = 256
LOG2E = 1.4426950408889634
NEG = -1e30
VMEM_LIMIT = 56 * 1024 * 1024

F32 = jnp.float32
BF16 = jnp.bfloat16

_ROLES = (["qa"] * 2 + ["ka"] * 2 + ["va"] * 2 + ["ga"] * 2 + ["qb"] * 3 + ["kb"] + ["vb"] + ["gb"] * 3
          + ["qn"] * 3 + ["kn"] * 3 + ["vn"] * 3 + ["gn"] * 3)
_NORM_D = {"qa": A_DK, "ka": A_DK, "qb": HEAD_DIM, "kb": HEAD_DIM, "qn": HEAD_DIM, "kn": HEAD_DIM}
_ROPE_ROLES = ("qa", "ka", "qb", "kb")


def _dot(a, b):
    return jnp.dot(a, b, preferred_element_type=F32)


def _silu(x):
    return x * (1.0 / (1.0 + jnp.exp(-x)))


def _ada_kernel(c_ref, w_ref, b_ref, o_ref):
    a = _silu(c_ref[...])
    o_ref[0] = jnp.dot(a, w_ref[0], preferred_element_type=F32,
                       precision=lax.Precision.HIGHEST) + b_ref[0]


def _ada_call(cc, w_ada, b_ada):
    n = cc.shape[0]
    nblk = 3 * D_MODEL // D_MODEL
    return pl.pallas_call(
        _ada_kernel,
        grid=(DEPTH, nblk),
        in_specs=[
            pl.BlockSpec((n, D_MODEL), lambda l, j: (0, 0)),
            pl.BlockSpec((1, D_MODEL, D_MODEL), lambda l, j: (l, 0, j)),
            pl.BlockSpec((1, 1, D_MODEL), lambda l, j: (l, 0, j)),
        ],
        out_specs=pl.BlockSpec((1, n, D_MODEL), lambda l, j: (l, 0, j)),
        out_shape=jax.ShapeDtypeStruct((DEPTH, n, 3 * D_MODEL), F32),
        compiler_params=pltpu.CompilerParams(
            dimension_semantics=("arbitrary", "arbitrary"), vmem_limit_bytes=VMEM_LIMIT),
        name="ada",
    )(cc, w_ada, b_ada)


def _proj_kernel(x_ref, mod_ref, ng_ref, w_ref, gain_ref, rope_ref, bd32_ref, bd64_ref,
                 qa_ref, qb_ref, qc_ref, kta_ref, ktb_ref, ktc_ref, va_ref, vb_ref, vc_ref,
                 ga_ref, gb_ref, gc_ref, *, rope, tm):
    x = x_ref[...]
    ms = jnp.mean(x * x, axis=-1, keepdims=True)
    xn = x * lax.rsqrt(ms + EPS) * ng_ref[...]
    h = xn * (1.0 + mod_ref[0, 1:2, :]) + mod_ref[0, 0:1, :]
    hb = h.astype(BF16)

    lane = lax.broadcasted_iota(jnp.int32, (1, LANES), 1)
    row = lax.broadcasted_iota(jnp.int32, (LANES, 1), 0)
    first_half = {A_DK: (lane % A_DK) < (A_DK // 2), HEAD_DIM: (lane % HEAD_DIM) < (HEAD_DIM // 2)}
    bd = {A_DK: bd32_ref, HEAD_DIM: bd64_ref}
    rope_idx = {A_DK: 0, HEAD_DIM: 2}
    count = {}

    for ch in range(IN_W // MXU_W):
        c0 = ch * MXU_W
        y = _dot(hb, w_ref[:, c0:c0 + MXU_W])
        roles = _ROLES[2 * ch:2 * ch + 2]
        d = _NORM_D.get(roles[0]) or _NORM_D.get(roles[1])
        if d is not None:
            ss = _dot((y * y).astype(BF16), bd[d][...])
            yn = y * lax.rsqrt(ss * (1.0 / d) + EPS) * gain_ref[:, c0:c0 + MXU_W]
        for half in range(2):
            role = roles[half]
            j = count.get(role, 0)
            count[role] = j + 1
            sl = slice(half * LANES, (half + 1) * LANES)
            if role in _NORM_D:
                z = yn[:, sl]
                if rope and role in _ROPE_ROLES:
                    dd = _NORM_D[role]
                    zsw = jnp.where(first_half[dd], pltpu.roll(z, LANES - dd // 2, 1),
                                    pltpu.roll(z, dd // 2, 1))
                    z = z * rope_ref[rope_idx[dd]] + zsw * rope_ref[rope_idx[dd] + 1]
            else:
                z = y[:, sl]
            osl = slice(j * LANES, (j + 1) * LANES)
            if role == "qa":
                qa_ref[:, osl] = z.astype(BF16)
            elif role == "qb":
                qb_ref[:, osl] = z.astype(BF16)
            elif role == "qn":
                qc_ref[:, osl] = z.astype(BF16)
            elif role == "ka":
                zt = z.T
                for s in range(LANES // A_DK):
                    kta_ref[0, 4 * j + s] = jnp.where(row // A_DK == s, zt, 0.0).astype(BF16)
            elif role == "kb":
                zt = z.T
                zst = pltpu.roll(z, HEAD_DIM, 1).T
                lo = row < HEAD_DIM
                ktb_ref[0, 0] = jnp.where(lo, zt, 0.0).astype(BF16)
                ktb_ref[0, 1] = jnp.where(lo, 0.0, zst).astype(BF16)
                ktb_ref[0, 2] = jnp.where(lo, zst, 0.0).astype(BF16)
                ktb_ref[0, 3] = jnp.where(lo, 0.0, zt).astype(BF16)
            elif role == "kn":
                zt = z.T
                lo = row < HEAD_DIM
                for cc in range(tm // MXU_W):
                    piece = zt[:, cc * MXU_W:(cc + 1) * MXU_W]
                    ktc_ref[0, 2 * j, cc] = jnp.where(lo, piece, 0.0).astype(BF16)
                    ktc_ref[0, 2 * j + 1, cc] = jnp.where(lo, 0.0, piece).astype(BF16)
            elif role == "va":
                va_ref[0, j] = z.astype(BF16)
            elif role == "vb":
                zs = pltpu.roll(z, HEAD_DIM, 1)
                lo = lane < HEAD_DIM
                vb_ref[0, 0] = jnp.where(lo, z, zs).astype(BF16)
                vb_ref[0, 1] = jnp.where(lo, zs, z).astype(BF16)
            elif role == "vn":
                vc_ref[0, j] = z.astype(BF16)
            elif role == "ga":
                ga_ref[:, osl] = _silu(z)
            elif role == "gb":
                gb_ref[:, osl] = _silu(z)
            elif role == "gn":
                gc_ref[:, osl] = _silu(z)


def _proj_call(xf, mod, ng, w, gain, rope_tab, bd32, bd64, *, nb, seq, tm, rope, mod_row):
    tpb = seq // tm
    nch = seq // MXU_W
    cpt = tm // MXU_W
    t_all = nb * seq
    if mod_row is None:
        mod_map = lambda i: (i // tpb, 0, 0)
    else:
        mod_map = lambda i: (mod_row, 0, 0)
    tok = lambda w_: pl.BlockSpec((tm, w_), lambda i: (i, 0))
    const2 = lambda shp: pl.BlockSpec(shp, lambda i: (0, 0))
    in_specs = [
        tok(D_MODEL),
        pl.BlockSpec((1, 3, D_MODEL), mod_map),
        const2((1, D_MODEL)),
        const2((D_MODEL, IN_W)),
        const2((1, IN_W)),
        pl.BlockSpec((4, tm, LANES), lambda i: (0, i % tpb, 0)),
        const2((MXU_W, MXU_W)),
        const2((MXU_W, MXU_W)),
    ]
    out_shape = [
        jax.ShapeDtypeStruct((t_all, 256), BF16),
        jax.ShapeDtypeStruct((t_all, 384), BF16),
        jax.ShapeDtypeStruct((t_all, 384), BF16),
        jax.ShapeDtypeStruct((nb, 8, LANES, seq), BF16),
        jax.ShapeDtypeStruct((nb, 4, LANES, seq), BF16),
        jax.ShapeDtypeStruct((nb, 6, nch, LANES, MXU_W), BF16),
        jax.ShapeDtypeStruct((nb, 2, seq, LANES), BF16),
        jax.ShapeDtypeStruct((nb, 2, seq, LANES), BF16),
        jax.ShapeDtypeStruct((nb, 3, seq, LANES), BF16),
        jax.ShapeDtypeStruct((t_all, 256), F32),
        jax.ShapeDtypeStruct((t_all, 384), F32),
        jax.ShapeDtypeStruct((t_all, 384), F32),
    ]
    out_specs = [
        tok(256), tok(384), tok(384),
        pl.BlockSpec((1, 8, LANES, tm), lambda i: (i // tpb, 0, 0, i % tpb)),
        pl.BlockSpec((1, 4, LANES, tm), lambda i: (i // tpb, 0, 0, i % tpb)),
        pl.BlockSpec((1, 6, cpt, LANES, MXU_W), lambda i: (i // tpb, 0, i % tpb, 0, 0)),
        pl.BlockSpec((1, 2, tm, LANES), lambda i: (i // tpb, 0, i % tpb, 0)),
        pl.BlockSpec((1, 2, tm, LANES), lambda i: (i // tpb, 0, i % tpb, 0)),
        pl.BlockSpec((1, 3, tm, LANES), lambda i: (i // tpb, 0, i % tpb, 0)),
        tok(256), tok(384), tok(384),
    ]
    return pl.pallas_call(
        functools.partial(_proj_kernel, rope=rope, tm=tm),
        grid=(t_all // tm,),
        in_specs=in_specs,
        out_specs=out_specs,
        out_shape=out_shape,
        compiler_params=pltpu.CompilerParams(
            dimension_semantics=("arbitrary",), vmem_limit_bytes=VMEM_LIMIT),
        name="proj_lat" if rope else "proj_ctx",
    )(xf, mod, ng, w, gain, rope_tab, bd32, bd64)


def _softmax_parts(scores):
    mx = scores[0].max(axis=-1, keepdims=True)
    for s in scores[1:]:
        mx = jnp.maximum(mx, s.max(axis=-1, keepdims=True))
    ps = [jnp.exp2(s - mx) for s in scores]
    tot = ps[0].sum(axis=-1, keepdims=True)
    for p in ps[1:]:
        tot = tot + p.sum(axis=-1, keepdims=True)
    return ps, tot


def _attn_a_kernel(*refs, has_lat, lam_init):
    if has_lat:
        lamp_ref, sub_ref, q_ref, g_ref, ktc_ref, vc_ref, kt_ref, v_ref, o_ref = refs
    else:
        lamp_ref, sub_ref, q_ref, g_ref, ktc_ref, vc_ref, o_ref = refs
    lp = lamp_ref[...]
    l1 = jnp.sum(lp[0:1] * lp[1:2], axis=-1, keepdims=True)
    l2 = jnp.sum(lp[2:3] * lp[3:4], axis=-1, keepdims=True)
    lam = jnp.exp(l1) - jnp.exp(l2) + lam_init
    lane = lax.broadcasted_iota(jnp.int32, (1, LANES), 1)
    for grp in range(2):
        ql = q_ref[:, grp * LANES:(grp + 1) * LANES]
        yg = None
        for half in range(2):
            h = 2 * grp + half
            parts = []
            for m in range(2):
                a = 2 * h + m
                sc = []
                if has_lat:
                    sc.append(_dot(ql, kt_ref[0, a]))
                sc.append(_dot(ql, ktc_ref[0, a]))
                parts.append(_softmax_parts(sc))
            (p0, t0), (p1, t1) = parts
            w0 = 1.0 / t0
            w1 = lam / t1
            mix = [(a0 * w0 - a1 * w1).astype(BF16) for a0, a1 in zip(p0, p1)]
            if has_lat:
                o = _dot(mix[0], v_ref[0, grp]) + _dot(mix[1], vc_ref[0, grp])
            else:
                o = _dot(mix[0], vc_ref[0, grp])
            o = jnp.where(lane // HEAD_DIM == half, o, 0.0)
            msq = jnp.sum(o * o, axis=-1, keepdims=True) * (1.0 / HEAD_DIM)
            yh = o * lax.rsqrt(msq + EPS)
            yg = yh if yg is None else yg + yh
        sl = slice(grp * LANES, (grp + 1) * LANES)
        yg = yg * sub_ref[...] * (1.0 - lam_init)
        o_ref[:, sl] = (yg * g_ref[:, sl]).astype(BF16)


def _attn_plain_kernel(*refs, has_lat, heads):
    if has_lat:
        q_ref, g_ref, ktc_ref, vc_ref, kt_ref, v_ref, o_ref = refs
    else:
        q_ref, g_ref, ktc_ref, vc_ref, o_ref = refs
    lane = lax.broadcasted_iota(jnp.int32, (1, LANES), 1)
    for grp, pair in enumerate(heads):
        ql = q_ref[:, grp * LANES:(grp + 1) * LANES]
        yg = None
        for half, (ki, vi) in enumerate(pair):
            sc = []
            if has_lat:
                sc.append(_dot(ql, kt_ref[0, ki]))
            sc.append(_dot(ql, ktc_ref[0, ki, 0] if len(ktc_ref.shape) == 5 else ktc_ref[0, ki]))
            ps, tot = _softmax_parts(sc)
            ps = [p.astype(BF16) for p in ps]
            if has_lat:
                o = _dot(ps[0], v_ref[0, vi]) + _dot(ps[1], vc_ref[0, vi])
            else:
                o = _dot(ps[0], vc_ref[0, vi])
            o = o * (1.0 / tot)
            yg = o if yg is None else jnp.where(lane // HEAD_DIM == half, o, yg)
        sl = slice(grp * LANES, (grp + 1) * LANES)
        o_ref[:, sl] = (yg * g_ref[:, sl]).astype(BF16)


def _attn_c_kernel(q_ref, g_ref, ktc_ref, vc_ref, kt_ref, v_ref, bias_ref, o_ref, *, nwin):
    t = pl.program_id(1)
    j0 = jnp.clip(t - 1, 0, kt_ref.shape[2] - nwin)
    k0 = pl.multiple_of(j0 * MXU_W, MXU_W)
    lane = lax.broadcasted_iota(jnp.int32, (1, LANES), 1)
    for grp in range(3):
        ql = q_ref[:, grp * LANES:(grp + 1) * LANES]
        yg = None
        for half in range(2):
            h = 2 * grp + half
            sc = [_dot(ql, kt_ref[0, h, j0 + c]) + bias_ref[0, h, :, c * MXU_W:(c + 1) * MXU_W]
                  for c in range(nwin)]
            sc.append(_dot(ql, ktc_ref[0, h, 0]))
            ps, tot = _softmax_parts(sc)
            ps = [p.astype(BF16) for p in ps]
            o = _dot(ps[nwin], vc_ref[0, grp])
            for c in range(nwin):
                o = o + _dot(ps[c], v_ref[0, grp, pl.ds(k0 + c * MXU_W, MXU_W), :])
            o = o * (1.0 / tot)
            yg = o if yg is None else jnp.where(lane // HEAD_DIM == half, o, yg)
        sl = slice(grp * LANES, (grp + 1) * LANES)
        o_ref[:, sl] = (yg * g_ref[:, sl]).astype(BF16)


def _attn_specs(width, tq, nq_tiles, kt_shape, v_shape, ktc_shape, vc_shape, has_lat):
    tokq = pl.BlockSpec((tq, width), lambda b, t: (b * nq_tiles + t, 0))

    def per_batch(shape):
        nd = len(shape)
        return pl.BlockSpec((1,) + tuple(shape[1:]), lambda b, t: (b,) + (0,) * (nd - 1))

    specs = [tokq, tokq, per_batch(ktc_shape), per_batch(vc_shape)]
    if has_lat:
        specs += [per_batch(kt_shape), per_batch(v_shape)]
    return specs, tokq


def _attn_a_call(lamp, sub, q, g, ktc, vc, kt, v, *, nb, sq, tq, lam_init):
    has_lat = kt is not None
    nqt = sq // tq
    specs, tokq = _attn_specs(256, tq, nqt, kt.shape if has_lat else None, v.shape if has_lat else None,
                              ktc.shape, vc.shape, has_lat)
    small = [pl.BlockSpec((8, LANES), lambda b, t: (0, 0)), pl.BlockSpec((1, LANES), lambda b, t: (0, 0))]
    args = [lamp, sub, q, g, ktc, vc] + ([kt, v] if has_lat else [])
    return pl.pallas_call(
        functools.partial(_attn_a_kernel, has_lat=has_lat, lam_init=lam_init),
        grid=(nb, nqt),
        in_specs=small + specs,
        out_specs=tokq,
        out_shape=jax.ShapeDtypeStruct((nb * sq, 256), BF16),
        compiler_params=pltpu.CompilerParams(
            dimension_semantics=("arbitrary", "arbitrary"), vmem_limit_bytes=VMEM_LIMIT),
        name="attn_a_lat" if has_lat else "attn_a_ctx",
    )(*args)


def _attn_plain_call(q, g, ktc, vc, kt, v, *, nb, sq, tq, heads, name):
    has_lat = kt is not None
    nqt = sq // tq
    specs, tokq = _attn_specs(384, tq, nqt, kt.shape if has_lat else None, v.shape if has_lat else None,
                              ktc.shape, vc.shape, has_lat)
    args = [q, g, ktc, vc] + ([kt, v] if has_lat else [])
    return pl.pallas_call(
        functools.partial(_attn_plain_kernel, has_lat=has_lat, heads=heads),
        grid=(nb, nqt),
        in_specs=specs,
        out_specs=tokq,
        out_shape=jax.ShapeDtypeStruct((nb * sq, 384), BF16),
        compiler_params=pltpu.CompilerParams(
            dimension_semantics=("arbitrary", "arbitrary"), vmem_limit_bytes=VMEM_LIMIT),
        name=name,
    )(*args)


_C_TQ = 4 * GRID_W
_C_NWIN = 3


def _attn_c_call(q, g, ktc, vc, kt, v, bias, *, nb, sq):
    nqt = sq // _C_TQ
    specs, tokq = _attn_specs(384, _C_TQ, nqt, kt.shape, v.shape, ktc.shape, vc.shape, True)
    kind = lambda b, t: (jnp.where(t == 0, 0, jnp.where(t == nqt - 1, 2, 1)), 0, 0, 0)
    specs.append(pl.BlockSpec((1, C_HEADS, _C_TQ, _C_NWIN * MXU_W), kind))
    return pl.pallas_call(
        functools.partial(_attn_c_kernel, nwin=_C_NWIN),
        grid=(nb, nqt),
        in_specs=specs,
        out_specs=tokq,
        out_shape=jax.ShapeDtypeStruct((nb * sq, 384), BF16),
        compiler_params=pltpu.CompilerParams(
            dimension_semantics=("arbitrary", "arbitrary"), vmem_limit_bytes=VMEM_LIMIT),
        name="attn_c_lat",
    )(q, g, ktc, vc, kt, v, bias)


def _out_kernel(x_ref, mod_ref, ya_ref, yb_ref, yc_ref, w_ref, o_ref):
    y = jnp.concatenate([ya_ref[...], yb_ref[...], yc_ref[...]], axis=1)
    o_ref[...] = x_ref[...] + mod_ref[0, 2:3, :] * _dot(y, w_ref[...])


def _out_call(xf, mod, ya, yb, yc, w, *, seq, tm, mod_row, name):
    tpb = seq // tm
    if mod_row is None:
        mod_map = lambda i: (i // tpb, 0, 0)
    else:
        mod_map = lambda i: (mod_row, 0, 0)
    tok = lambda w_: pl.BlockSpec((tm, w_), lambda i: (i, 0))
    return pl.pallas_call(
        _out_kernel,
        grid=(xf.shape[0] // tm,),
        in_specs=[tok(D_MODEL), pl.BlockSpec((1, 3, D_MODEL), mod_map), tok(256), tok(384), tok(384),
                  pl.BlockSpec((D_MODEL, D_MODEL), lambda i: (0, 0))],
        out_specs=tok(D_MODEL),
        out_shape=jax.ShapeDtypeStruct(xf.shape, F32),
        compiler_params=pltpu.CompilerParams(
            dimension_semantics=("arbitrary",), vmem_limit_bytes=VMEM_LIMIT),
        name=name,
    )(xf, mod, ya, yb, yc, w)


def _rope_tables(seq):
    t = np.arange(seq)
    row = (t // GRID_W).astype(np.float32)
    col = (t % GRID_W).astype(np.float32)
    out = []
    for d in (A_DK, HEAD_DIM):
        nf = d // 4
        inv = (np.float32(ROPE_THETA) ** (-np.arange(nf, dtype=np.float32) / np.float32(nf))).astype(np.float32)
        ang = np.concatenate([row[:, None] * inv, col[:, None] * inv], axis=-1).astype(np.float32)
        cos = np.cos(ang).astype(np.float32)
        sin = np.sin(ang).astype(np.float32)
        cos_h = np.concatenate([cos, cos], axis=-1)
        sin_h = np.concatenate([-sin, sin], axis=-1)
        out.append(np.tile(cos_h, (1, LANES // d)))
        out.append(np.tile(sin_h, (1, LANES // d)))
    return np.stack(out).astype(np.float32)


def _block_diag(d):
    i = np.arange(MXU_W)
    return (i[:, None] // d == i[None, :] // d).astype(np.float32)


def _bias_index(rows):
    nqt = rows // 4
    kh = min(WIN_H, rows)
    dr_all, dc_all, ok_all = [], [], []
    for t in (0, 1, nqt - 1):
        ks = int(np.clip(4 * t - 4, 0, rows - 4 * _C_NWIN))
        qi = np.arange(_C_TQ)
        kj = np.arange(_C_NWIN * MXU_W)
        qr = 4 * t + qi // GRID_W
        qc = qi % GRID_W
        kr = ks + kj // GRID_W
        kc = kj % GRID_W
        r0 = np.clip(qr - kh // 2, 0, rows - kh)
        c0 = np.clip(qc - WIN_W // 2, 0, GRID_W - WIN_W)
        ok = ((kr[None, :] >= r0[:, None]) & (kr[None, :] < r0[:, None] + kh)
              & (kc[None, :] >= c0[:, None]) & (kc[None, :] < c0[:, None] + WIN_W))
        dr = np.clip(kr[None, :] - qr[:, None] + (WIN_H - 1), 0, 2 * WIN_H - 2)
        dc = np.clip(kc[None, :] - qc[:, None] + (WIN_W - 1), 0, 2 * WIN_W - 2)
        dr_all.append(dr)
        dc_all.append(dc)
        ok_all.append(ok)
    return np.stack(dr_all), np.stack(dc_all), np.stack(ok_all)


_B_HEADS_MAP = tuple(tuple(((h // 3) * 2 + (h % 2), h // 3) for h in (2 * g, 2 * g + 1)) for g in range(3))
_C_HEADS_MAP = tuple(tuple((h, g) for h in (2 * g, 2 * g + 1)) for g in range(3))


def kernel(x, c, ctx, c_ctx, norm_g, w_ada, b_ada, w_in, w_out, diff_q_norm, diff_k_norm, lambda_q1, lambda_k1,
           lambda_q2, lambda_k2, diff_subln, gqa_q_norm, gqa_k_norm, nat_q_norm, nat_k_norm, nat_rpb):
    nb, seq, _ = x.shape
    n_ctx = ctx.shape[1]
    rows = seq // GRID_W

    cc = jnp.concatenate([c, c_ctx[None, :], jnp.zeros((16 - nb - 1, D_MODEL), F32)], axis=0)
    mod_all = _ada_call(cc, w_ada, b_ada.reshape(DEPTH, 1, 3 * D_MODEL)).reshape(DEPTH, 16, 3, D_MODEL)

    rope_lat = jnp.asarray(_rope_tables(seq))
    rope_ctx = jnp.zeros((4, n_ctx, LANES), F32)
    bd32 = jnp.asarray(_block_diag(A_DK), BF16)
    bd64 = jnp.asarray(_block_diag(HEAD_DIM), BF16)
    dr_i, dc_i, ok_i = _bias_index(rows)

    xf = x.reshape(nb * seq, D_MODEL)
    cf = ctx.reshape(nb * n_ctx, D_MODEL)
    ones128 = jnp.ones((LANES,), F32)

    for l in range(DEPTH):
        lam_init = 0.8 - 0.6 * math.exp(-0.3 * l)
        last = l == DEPTH - 1
        w_l = w_in[l].astype(BF16)
        wo_l = w_out[l].astype(BF16)
        mod = mod_all[l]
        ng = norm_g[l].reshape(1, D_MODEL)
        sa = A_DK ** -0.5 * LOG2E
        sb = HEAD_DIM ** -0.5 * LOG2E
        gain = jnp.concatenate([
            jnp.tile(diff_q_norm[l], 8) * sa, jnp.tile(diff_k_norm[l], 8), ones128, ones128, ones128, ones128,
            jnp.tile(gqa_q_norm[l], 6) * sb, jnp.tile(gqa_k_norm[l], 2), ones128, ones128, ones128, ones128,
            jnp.tile(nat_q_norm[l], 6) * sb, jnp.tile(nat_k_norm[l], 6),
            ones128, ones128, ones128, ones128, ones128, ones128]).reshape(1, IN_W)
        lamp = jnp.zeros((8, LANES), F32)
        lamp = lamp.at[0, :A_DK].set(lambda_q1[l]).at[1, :A_DK].set(lambda_k1[l])
        lamp = lamp.at[2, :A_DK].set(lambda_q2[l]).at[3, :A_DK].set(lambda_k2[l])
        sub = jnp.tile(diff_subln[l], 2).reshape(1, LANES)
        bias = jnp.where(ok_i[None], nat_rpb[l][:, dr_i, dc_i] * LOG2E, NEG)
        bias = jnp.transpose(bias, (1, 0, 2, 3))

        lat = _proj_call(xf, mod, ng, w_l, gain, rope_lat, bd32, bd64,
                         nb=nb, seq=seq, tm=512, rope=True, mod_row=None)
        cx = _proj_call(cf, mod, ng, w_l, gain, rope_ctx, bd32, bd64,
                        nb=nb, seq=n_ctx, tm=n_ctx, rope=False, mod_row=nb)
        qa, qb, qc, kta, ktb, ktc, va, vb, vc, ga, gb, gc = lat
        qa_c, qb_c, qc_c, kta_c, ktb_c, ktc_c, va_c, vb_c, vc_c, ga_c, gb_c, gc_c = cx

        ya = _attn_a_call(lamp, sub, qa, ga, kta_c, va_c, kta, va, nb=nb, sq=seq, tq=256, lam_init=lam_init)
        yb = _attn_plain_call(qb, gb, ktb_c, vb_c, ktb, vb, nb=nb, sq=seq, tq=256, heads=_B_HEADS_MAP,
                              name="attn_b_lat")
        yc = _attn_c_call(qc, gc, ktc_c, vc_c, ktc, vc, bias, nb=nb, sq=seq)
        xf_new = _out_call(xf, mod, ya, yb, yc, wo_l, seq=seq, tm=512, mod_row=None, name="out_lat")

        if not last:
            ya_c = _attn_a_call(lamp, sub, qa_c, ga_c, kta_c, va_c, None, None, nb=nb, sq=n_ctx, tq=n_ctx,
                                lam_init=lam_init)
            yb_c = _attn_plain_call(qb_c, gb_c, ktb_c, vb_c, None, None, nb=nb, sq=n_ctx, tq=n_ctx,
                                    heads=_B_HEADS_MAP, name="attn_b_ctx")
            yc_c = _attn_plain_call(qc_c, gc_c, ktc_c, vc_c, None, None, nb=nb, sq=n_ctx, tq=n_ctx,
                                    heads=_C_HEADS_MAP, name="attn_c_ctx")
            cf = _out_call(cf, mod, ya_c, yb_c, yc_c, wo_l, seq=n_ctx, tm=n_ctx, mod_row=nb, name="out_ctx")
        xf = xf_new

    return xf.reshape(nb, seq, D_MODEL)
```

```python
import functools
import math

import numpy as np
import jax
import jax.numpy as jnp
from jax import lax
from jax.experimental import pallas as pl
from jax.experimental.pallas import tpu as pltpu

D_MODEL = 1024
DEPTH = 4
GRID_W = 64
HEAD_DIM = 64
A_HEADS = 4
A_DK = 32
B_HEADS = 6
B_KV_HEADS = 2
C_HEADS = 6
WIN_H = 8
WIN_W = 16
ROPE_THETA = 10000.0
EPS = 1e-6
IN_W = 3584

LANES = 128
MXU_W = 256
LOG2E = 1.4426950408889634
NEG = -1e30
VMEM_LIMIT = 56 * 1024 * 1024

F32 = jnp.float32
BF16 = jnp.bfloat16

_ROLES = (["qa"] * 2 + ["ka"] * 2 + ["va"] * 2 + ["ga"] * 2 + ["qb"] * 3 + ["kb"] + ["vb"] + ["gb"] * 3
          + ["qn"] * 3 + ["kn"] * 3 + ["vn"] * 3 + ["gn"] * 3)
_NORM_D = {"qa": A_DK, "ka": A_DK, "qb": HEAD_DIM, "kb": HEAD_DIM, "qn": HEAD_DIM, "kn": HEAD_DIM}
_ROPE_ROLES = ("qa", "ka", "qb", "kb")


def _dot(a, b):
    return jnp.dot(a, b, preferred_element_type=F32)


def _silu(x):
    return x * (1.0 / (1.0 + jnp.exp(-x)))


def _ada_kernel(c_ref, w_ref, b_ref, o_ref):
    a = _silu(c_ref[...])
    o_ref[0] = jnp.dot(a, w_ref[0], preferred_element_type=F32,
                       precision=lax.Precision.HIGHEST) + b_ref[0]


def _ada_call(cc, w_ada, b_ada):
    n = cc.shape[0]
    nblk = 3 * D_MODEL // D_MODEL
    return pl.pallas_call(
        _ada_kernel,
        grid=(DEPTH, nblk),
        in_specs=[
            pl.BlockSpec((n, D_MODEL), lambda l, j: (0, 0)),
            pl.BlockSpec((1, D_MODEL, D_MODEL), lambda l, j: (l, 0, j)),
            pl.BlockSpec((1, 1, D_MODEL), lambda l, j: (l, 0, j)),
        ],
        out_specs=pl.BlockSpec((1, n, D_MODEL), lambda l, j: (l, 0, j)),
        out_shape=jax.ShapeDtypeStruct((DEPTH, n, 3 * D_MODEL), F32),
        compiler_params=pltpu.CompilerParams(
            dimension_semantics=("arbitrary", "arbitrary"), vmem_limit_bytes=VMEM_LIMIT),
        name="ada",
    )(cc, w_ada, b_ada)


def _proj_kernel(x_ref, mod_ref, ng_ref, w_ref, gain_ref, rope_ref, bd32_ref, bd64_ref,
                 qa_ref, qb_ref, qc_ref, kta_ref, ktb_ref, ktc_ref, va_ref, vb_ref, vc_ref,
                 ga_ref, gb_ref, gc_ref, *, rope, tm):
    x = x_ref[...]
    ms = jnp.mean(x * x, axis=-1, keepdims=True)
    xn = x * lax.rsqrt(ms + EPS) * ng_ref[...]
    h = xn * (1.0 + mod_ref[0, 1:2, :]) + mod_ref[0, 0:1, :]
    hb = h.astype(BF16)

    lane = lax.broadcasted_iota(jnp.int32, (1, LANES), 1)
    row = lax.broadcasted_iota(jnp.int32, (LANES, 1), 0)
    first_half = {A_DK: (lane % A_DK) < (A_DK // 2), HEAD_DIM: (lane % HEAD_DIM) < (HEAD_DIM // 2)}
    bd = {A_DK: bd32_ref, HEAD_DIM: bd64_ref}
    rope_idx = {A_DK: 0, HEAD_DIM: 2}
    count = {}

    for ch in range(IN_W // MXU_W):
        c0 = ch * MXU_W
        y = _dot(hb, w_ref[:, c0:c0 + MXU_W])
        roles = _ROLES[2 * ch:2 * ch + 2]
        d = _NORM_D.get(roles[0]) or _NORM_D.get(roles[1])
        if d is not None:
            ss = _dot((y * y).astype(BF16), bd[d][...])
            yn = y * lax.rsqrt(ss * (1.0 / d) + EPS) * gain_ref[:, c0:c0 + MXU_W]
        for half in range(2):
            role = roles[half]
            j = count.get(role, 0)
            count[role] = j + 1
            sl = slice(half * LANES, (half + 1) * LANES)
            if role in _NORM_D:
                z = yn[:, sl]
                if rope and role in _ROPE_ROLES:
                    dd = _NORM_D[role]
                    zsw = jnp.where(first_half[dd], pltpu.roll(z, LANES - dd // 2, 1),
                                    pltpu.roll(z, dd // 2, 1))
                    z = z * rope_ref[rope_idx[dd]] + zsw * rope_ref[rope_idx[dd] + 1]
            else:
                z = y[:, sl]
            osl = slice(j * LANES, (j + 1) * LANES)
            if role == "qa":
                qa_ref[:, osl] = z.astype(BF16)
            elif role == "qb":
                qb_ref[:, osl] = z.astype(BF16)
            elif role == "qn":
                qc_ref[:, osl] = z.astype(BF16)
            elif role == "ka":
                zt = z.T
                for s in range(LANES // A_DK):
                    kta_ref[0, 4 * j + s] = jnp.where(row // A_DK == s, zt, 0.0).astype(BF16)
            elif role == "kb":
                zt = z.T
                zst = pltpu.roll(z, HEAD_DIM, 1).T
                lo = row < HEAD_DIM
                ktb_ref[0, 0] = jnp.where(lo, zt, 0.0).astype(BF16)
                ktb_ref[0, 1] = jnp.where(lo, 0.0, zst).astype(BF16)
                ktb_ref[0, 2] = jnp.where(lo, zst, 0.0).astype(BF16)
                ktb_ref[0, 3] = jnp.where(lo, 0.0, zt).astype(BF16)
            elif role == "kn":
                zt = z.T
                lo = row < HEAD_DIM
                for cc in range(tm // MXU_W):
                    piece = zt[:, cc * MXU_W:(cc + 1) * MXU_W]
                    ktc_ref[0, 2 * j, cc] = jnp.where(lo, piece, 0.0).astype(BF16)
                    ktc_ref[0, 2 * j + 1, cc] = jnp.where(lo, 0.0, piece).astype(BF16)
            elif role == "va":
                va_ref[0, j] = z.astype(BF16)
            elif role == "vb":
                zs = pltpu.roll(z, HEAD_DIM, 1)
                lo = lane < HEAD_DIM
                vb_ref[0, 0] = jnp.where(lo, z, zs).astype(BF16)
                vb_ref[0, 1] = jnp.where(lo, zs, z).astype(BF16)
            elif role == "vn":
                vc_ref[0, j] = z.astype(BF16)
            elif role == "ga":
                ga_ref[:, osl] = _silu(z)
            elif role == "gb":
                gb_ref[:, osl] = _silu(z)
            elif role == "gn":
                gc_ref[:, osl] = _silu(z)


def _proj_call(xf, mod, ng, w, gain, rope_tab, bd32, bd64, *, nb, seq, tm, rope, mod_row):
    tpb = seq // tm
    nch = seq // MXU_W
    cpt = tm // MXU_W
    t_all = nb * seq
    if mod_row is None:
        mod_map = lambda i: (i // tpb, 0, 0)
    else:
        mod_map = lambda i: (mod_row, 0, 0)
    tok = lambda w_: pl.BlockSpec((tm, w_), lambda i: (i, 0))
    const2 = lambda shp: pl.BlockSpec(shp, lambda i: (0, 0))
    in_specs = [
        tok(D_MODEL),
        pl.BlockSpec((1, 3, D_MODEL), mod_map),
        const2((1, D_MODEL)),
        const2((D_MODEL, IN_W)),
        const2((1, IN_W)),
        pl.BlockSpec((4, tm, LANES), lambda i: (0, i % tpb, 0)),
        const2((MXU_W, MXU_W)),
        const2((MXU_W, MXU_W)),
    ]
    out_shape = [
        jax.ShapeDtypeStruct((t_all, 256), BF16),
        jax.ShapeDtypeStruct((t_all, 384), BF16),
        jax.ShapeDtypeStruct((t_all, 384), BF16),
        jax.ShapeDtypeStruct((nb, 8, LANES, seq), BF16),
        jax.ShapeDtypeStruct((nb, 4, LANES, seq), BF16),
        jax.ShapeDtypeStruct((nb, 6, nch, LANES, MXU_W), BF16),
        jax.ShapeDtypeStruct((nb, 2, seq, LANES), BF16),
        jax.ShapeDtypeStruct((nb, 2, seq, LANES), BF16),
        jax.ShapeDtypeStruct((nb, 3, seq, LANES), BF16),
        jax.ShapeDtypeStruct((t_all, 256), F32),
        jax.ShapeDtypeStruct((t_all, 384), F32),
        jax.ShapeDtypeStruct((t_all, 384), F32),
    ]
    out_specs = [
        tok(256), tok(384), tok(384),
        pl.BlockSpec((1, 8, LANES, tm), lambda i: (i // tpb, 0, 0, i % tpb)),
        pl.BlockSpec((1, 4, LANES, tm), lambda i: (i // tpb, 0, 0, i % tpb)),
        pl.BlockSpec((1, 6, cpt, LANES, MXU_W), lambda i: (i // tpb, 0, i % tpb, 0, 0)),
        pl.BlockSpec((1, 2, tm, LANES), lambda i: (i // tpb, 0, i % tpb, 0)),
        pl.BlockSpec((1, 2, tm, LANES), lambda i: (i // tpb, 0, i % tpb, 0)),
        pl.BlockSpec((1, 3, tm, LANES), lambda i: (i // tpb, 0, i % tpb, 0)),
        tok(256), tok(384), tok(384),
    ]
    return pl.pallas_call(
        functools.partial(_proj_kernel, rope=rope, tm=tm),
        grid=(t_all // tm,),
        in_specs=in_specs,
        out_specs=out_specs,
        out_shape=out_shape,
        compiler_params=pltpu.CompilerParams(
            dimension_semantics=("arbitrary",), vmem_limit_bytes=VMEM_LIMIT),
        name="proj_lat" if rope else "proj_ctx",
    )(xf, mod, ng, w, gain, rope_tab, bd32, bd64)


def _softmax_parts(scores):
    mx = scores[0].max(axis=-1, keepdims=True)
    for s in scores[1:]:
        mx = jnp.maximum(mx, s.max(axis=-1, keepdims=True))
    ps = [jnp.exp2(s - mx) for s in scores]
    tot = ps[0].sum(axis=-1, keepdims=True)
    for p in ps[1:]:
        tot = tot + p.sum(axis=-1, keepdims=True)
    return ps, tot


def _attn_a_kernel(*refs, has_lat, lam_init):
    if has_lat:
        lamp_ref, sub_ref, q_ref, g_ref, ktc_ref, vc_ref, kt_ref, v_ref, o_ref = refs
    else:
        lamp_ref, sub_ref, q_ref, g_ref, ktc_ref, vc_ref, o_ref = refs
    lp = lamp_ref[...]
    l1 = jnp.sum(lp[0:1] * lp[1:2], axis=-1, keepdims=True)
    l2 = jnp.sum(lp[2:3] * lp[3:4], axis=-1, keepdims=True)
    lam = jnp.exp(l1) - jnp.exp(l2) + lam_init
    lane = lax.broadcasted_iota(jnp.int32, (1, LANES), 1)
    for grp in range(2):
        ql = q_ref[:, grp * LANES:(grp + 1) * LANES]
        yg = None
        for half in range(2):
            h = 2 * grp + half
            parts = []
            for m in range(2):
                a = 2 * h + m
                sc = []
                if has_lat:
                    sc.append(_dot(ql, kt_ref[0, a]))
                sc.append(_dot(ql, ktc_ref[0, a]))
                parts.append(_softmax_parts(sc))
            (p0, t0), (p1, t1) = parts
            w0 = 1.0 / t0
            w1 = lam / t1
            mix = [(a0 * w0 - a1 * w1).astype(BF16) for a0, a1 in zip(p0, p1)]
            if has_lat:
                o = _dot(mix[0], v_ref[0, grp]) + _dot(mix[1], vc_ref[0, grp])
            else:
                o = _dot(mix[0], vc_ref[0, grp])
            o = jnp.where(lane // HEAD_DIM == half, o, 0.0)
            msq = jnp.sum(o * o, axis=-1, keepdims=True) * (1.0 / HEAD_DIM)
            yh = o * lax.rsqrt(msq + EPS)
            yg = yh if yg is None else yg + yh
        sl = slice(grp * LANES, (grp + 1) * LANES)
        yg = yg * sub_ref[...] * (1.0 - lam_init)
        o_ref[:, sl] = (yg * g_ref[:, sl]).astype(BF16)


def _attn_plain_kernel(*refs, has_lat, heads):
    if has_lat:
        q_ref, g_ref, ktc_ref, vc_ref, kt_ref, v_ref, o_ref = refs
    else:
        q_ref, g_ref, ktc_ref, vc_ref, o_ref = refs
    lane = lax.broadcasted_iota(jnp.int32, (1, LANES), 1)
    for grp, pair in enumerate(heads):
        ql = q_ref[:, grp * LANES:(grp + 1) * LANES]
        yg = None
        for half, (ki, vi) in enumerate(pair):
            sc = []
            if has_lat:
                sc.append(_dot(ql, kt_ref[0, ki]))
            sc.append(_dot(ql, ktc_ref[0, ki, 0] if len(ktc_ref.shape) == 5 else ktc_ref[0, ki]))
            ps, tot = _softmax_parts(sc)
            ps = [p.astype(BF16) for p in ps]
            if has_lat:
                o = _dot(ps[0], v_ref[0, vi]) + _dot(ps[1], vc_ref[0, vi])
            else:
                o = _dot(ps[0], vc_ref[0, vi])
            o = o * (1.0 / tot)
            yg = o if yg is None else jnp.where(lane // HEAD_DIM == half, o, yg)
        sl = slice(grp * LANES, (grp + 1) * LANES)
        o_ref[:, sl] = (yg * g_ref[:, sl]).astype(BF16)


def _attn_c_kernel(q_ref, g_ref, ktc_ref, vc_ref, kt_ref, v_ref, bias_ref, o_ref, *, nwin):
    t = pl.program_id(1)
    j0 = jnp.clip(t - 1, 0, kt_ref.shape[2] - nwin)
    k0 = pl.multiple_of(j0 * MXU_W, MXU_W)
    lane = lax.broadcasted_iota(jnp.int32, (1, LANES), 1)
    for grp in range(3):
        ql = q_ref[:, grp * LANES:(grp + 1) * LANES]
        yg = None
        for half in range(2):
            h = 2 * grp + half
            sc = [_dot(ql, kt_ref[0, h, j0 + c]) + bias_ref[0, h, :, c * MXU_W:(c + 1) * MXU_W]
                  for c in range(nwin)]
            sc.append(_dot(ql, ktc_ref[0, h, 0]))
            ps, tot = _softmax_parts(sc)
            ps = [p.astype(BF16) for p in ps]
            o = _dot(ps[nwin], vc_ref[0, grp])
            for c in range(nwin):
                o = o + _dot(ps[c], v_ref[0, grp, pl.ds(k0 + c * MXU_W, MXU_W), :])
            o = o * (1.0 / tot)
            yg = o if yg is None else jnp.where(lane // HEAD_DIM == half, o, yg)
        sl = slice(grp * LANES, (grp + 1) * LANES)
        o_ref[:, sl] = (yg * g_ref[:, sl]).astype(BF16)


def _attn_specs(width, tq, nq_tiles, kt_shape, v_shape, ktc_shape, vc_shape, has_lat):
    tokq = pl.BlockSpec((tq, width), lambda b, t: (b * nq_tiles + t, 0))

    def per_batch(shape):
        nd = len(shape)
        return pl.BlockSpec((1,) + tuple(shape[1:]), lambda b, t: (b,) + (0,) * (nd - 1))

    specs = [tokq, tokq, per_batch(ktc_shape), per_batch(vc_shape)]
    if has_lat:
        specs += [per_batch(kt_shape), per_batch(v_shape)]
    return specs, tokq


def _attn_a_call(lamp, sub, q, g, ktc, vc, kt, v, *, nb, sq, tq, lam_init):
    has_lat = kt is not None
    nqt = sq // tq
    specs, tokq = _attn_specs(256, tq, nqt, kt.shape if has_lat else None, v.shape if has_lat else None,
                              ktc.shape, vc.shape, has_lat)
    small = [pl.BlockSpec((8, LANES), lambda b, t: (0, 0)), pl.BlockSpec((1, LANES), lambda b, t: (0, 0))]
    args = [lamp, sub, q, g, ktc, vc] + ([kt, v] if has_lat else [])
    return pl.pallas_call(
        functools.partial(_attn_a_kernel, has_lat=has_lat, lam_init=lam_init),
        grid=(nb, nqt),
        in_specs=small + specs,
        out_specs=tokq,
        out_shape=jax.ShapeDtypeStruct((nb * sq, 256), BF16),
        compiler_params=pltpu.CompilerParams(
            dimension_semantics=("arbitrary", "arbitrary"), vmem_limit_bytes=VMEM_LIMIT),
        name="attn_a_lat" if has_lat else "attn_a_ctx",
    )(*args)


def _attn_plain_call(q, g, ktc, vc, kt, v, *, nb, sq, tq, heads, name):
    has_lat = kt is not None
    nqt = sq // tq
    specs, tokq = _attn_specs(384, tq, nqt, kt.shape if has_lat else None, v.shape if has_lat else None,
                              ktc.shape, vc.shape, has_lat)
    args = [q, g, ktc, vc] + ([kt, v] if has_lat else [])
    return pl.pallas_call(
        functools.partial(_attn_plain_kernel, has_lat=has_lat, heads=heads),
        grid=(nb, nqt),
        in_specs=specs,
        out_specs=tokq,
        out_shape=jax.ShapeDtypeStruct((nb * sq, 384), BF16),
        compiler_params=pltpu.CompilerParams(
            dimension_semantics=("arbitrary", "arbitrary"), vmem_limit_bytes=VMEM_LIMIT),
        name=name,
    )(*args)


_C_TQ = 4 * GRID_W
_C_NWIN = 3


def _attn_c_call(q, g, ktc, vc, kt, v, bias, *, nb, sq):
    nqt = sq // _C_TQ
    specs, tokq = _attn_specs(384, _C_TQ, nqt, kt.shape, v.shape, ktc.shape, vc.shape, True)
    kind = lambda b, t: (jnp.where(t == 0, 0, jnp.where(t == nqt - 1, 2, 1)), 0, 0, 0)
    specs.append(pl.BlockSpec((1, C_HEADS, _C_TQ, _C_NWIN * MXU_W), kind))
    return pl.pallas_call(
        functools.partial(_attn_c_kernel, nwin=_C_NWIN),
        grid=(nb, nqt),
        in_specs=specs,
        out_specs=tokq,
        out_shape=jax.ShapeDtypeStruct((nb * sq, 384), BF16),
        compiler_params=pltpu.CompilerParams(
            dimension_semantics=("arbitrary", "arbitrary"), vmem_limit_bytes=VMEM_LIMIT),
        name="attn_c_lat",
    )(q, g, ktc, vc, kt, v, bias)


_RPB_H = 2 * WIN_H - 1
_RPB_W = 2 * WIN_W - 1


def _bias_kernel(rpb_ref, o_ref, *, rows):
    base = pl.program_id(0) * (_RPB_H * _RPB_W)
    lane = lax.broadcasted_iota(jnp.int32, (GRID_W, LANES), 1)
    qc = lax.broadcasted_iota(jnp.int32, (GRID_W, LANES), 0)
    kc = lane % GRID_W
    dcol = kc - qc + (WIN_W - 1)
    c0 = jnp.clip(qc - WIN_W // 2, 0, GRID_W - WIN_W)
    ok_c = (kc >= c0) & (kc < c0 + WIN_W)
    lo = lane < GRID_W
    kh = min(WIN_H, rows)
    nqt = rows // 4
    tables = []
    for dr in range(_RPB_H):
        acc = jnp.full((GRID_W, LANES), rpb_ref[base + dr * _RPB_W], F32)
        for dc in range(1, _RPB_W):
            acc = jnp.where(dcol >= dc, rpb_ref[base + dr * _RPB_W + dc], acc)
        tables.append(acc * LOG2E)
    for kind, t in enumerate((0, 1, nqt - 1)):
        ks = min(max(4 * t - 4, 0), rows - 4 * _C_NWIN)
        for i in range(4):
            qr = 4 * t + i
            r0 = min(max(qr - kh // 2, 0), rows - kh)
            for jj in range(_C_NWIN * 2):
                kra = ks + 2 * jj
                ok_a = r0 <= kra < r0 + kh
                ok_b = r0 <= kra + 1 < r0 + kh
                dra = kra - qr + (WIN_H - 1)
                if ok_a and ok_b:
                    piece = jnp.where(ok_c, jnp.where(lo, tables[dra], tables[dra + 1]), NEG)
                elif ok_a:
                    piece = jnp.where(ok_c & lo, tables[dra], NEG)
                elif ok_b:
                    piece = jnp.where(ok_c & jnp.logical_not(lo), tables[dra + 1], NEG)
                else:
                    piece = jnp.full((GRID_W, LANES), NEG, F32)
                o_ref[kind, 0, i * GRID_W:(i + 1) * GRID_W, jj * LANES:(jj + 1) * LANES] = piece


def _bias_call(rpb_flat, rows):
    return pl.pallas_call(
        functools.partial(_bias_kernel, rows=rows),
        grid=(C_HEADS,),
        in_specs=[pl.BlockSpec(memory_space=pltpu.SMEM)],
        out_specs=pl.BlockSpec((3, 1, _C_TQ, _C_NWIN * MXU_W), lambda h: (0, h, 0, 0)),
        out_shape=jax.ShapeDtypeStruct((3, C_HEADS, _C_TQ, _C_NWIN * MXU_W), F32),
        compiler_params=pltpu.CompilerParams(
            dimension_semantics=("arbitrary",), vmem_limit_bytes=VMEM_LIMIT),
        name="nat_bias",
    )(rpb_flat)


def _out_kernel(x_ref, mod_ref, ya_ref, yb_ref, yc_ref, w_ref, o_ref):
    y = jnp.concatenate([ya_ref[...], yb_ref[...], yc_ref[...]], axis=1)
    o_ref[...] = x_ref[...] + mod_ref[0, 2:3, :] * _dot(y, w_ref[...])


def _out_call(xf, mod, ya, yb, yc, w, *, seq, tm, mod_row, name):
    tpb = seq // tm
    if mod_row is None:
        mod_map = lambda i: (i // tpb, 0, 0)
    else:
        mod_map = lambda i: (mod_row, 0, 0)
    tok = lambda w_: pl.BlockSpec((tm, w_), lambda i: (i, 0))
    return pl.pallas_call(
        _out_kernel,
        grid=(xf.shape[0] // tm,),
        in_specs=[tok(D_MODEL), pl.BlockSpec((1, 3, D_MODEL), mod_map), tok(256), tok(384), tok(384),
                  pl.BlockSpec((D_MODEL, D_MODEL), lambda i: (0, 0))],
        out_specs=tok(D_MODEL),
        out_shape=jax.ShapeDtypeStruct(xf.shape, F32),
        compiler_params=pltpu.CompilerParams(
            dimension_semantics=("arbitrary",), vmem_limit_bytes=VMEM_LIMIT),
        name=name,
    )(xf, mod, ya, yb, yc, w)


def _rope_tables(seq):
    t = np.arange(seq)
    row = (t // GRID_W).astype(np.float32)
    col = (t % GRID_W).astype(np.float32)
    out = []
    for d in (A_DK, HEAD_DIM):
        nf = d // 4
        inv = (np.float32(ROPE_THETA) ** (-np.arange(nf, dtype=np.float32) / np.float32(nf))).astype(np.float32)
        ang = np.concatenate([row[:, None] * inv, col[:, None] * inv], axis=-1).astype(np.float32)
        cos = np.cos(ang).astype(np.float32)
        sin = np.sin(ang).astype(np.float32)
        cos_h = np.concatenate([cos, cos], axis=-1)
        sin_h = np.concatenate([-sin, sin], axis=-1)
        out.append(np.tile(cos_h, (1, LANES // d)))
        out.append(np.tile(sin_h, (1, LANES // d)))
    return np.stack(out).astype(np.float32)


def _block_diag(d):
    i = np.arange(MXU_W)
    return (i[:, None] // d == i[None, :] // d).astype(np.float32)


_B_HEADS_MAP = tuple(tuple(((h // 3) * 2 + (h % 2), h // 3) for h in (2 * g, 2 * g + 1)) for g in range(3))
_C_HEADS_MAP = tuple(tuple((h, g) for h in (2 * g, 2 * g + 1)) for g in range(3))


def kernel(x, c, ctx, c_ctx, norm_g, w_ada, b_ada, w_in, w_out, diff_q_norm, diff_k_norm, lambda_q1, lambda_k1,
           lambda_q2, lambda_k2, diff_subln, gqa_q_norm, gqa_k_norm, nat_q_norm, nat_k_norm, nat_rpb):
    nb, seq, _ = x.shape
    n_ctx = ctx.shape[1]
    rows = seq // GRID_W

    cc = jnp.concatenate([c, c_ctx[None, :], jnp.zeros((16 - nb - 1, D_MODEL), F32)], axis=0)
    mod_all = _ada_call(cc, w_ada, b_ada.reshape(DEPTH, 1, 3 * D_MODEL)).reshape(DEPTH, 16, 3, D_MODEL)

    rope_lat = jnp.asarray(_rope_tables(seq))
    rope_ctx = jnp.zeros((4, n_ctx, LANES), F32)
    bd32 = jnp.asarray(_block_diag(A_DK), BF16)
    bd64 = jnp.asarray(_block_diag(HEAD_DIM), BF16)

    xf = x.reshape(nb * seq, D_MODEL)
    cf = ctx.reshape(nb * n_ctx, D_MODEL)
    ones128 = jnp.ones((LANES,), F32)

    for l in range(DEPTH):
        lam_init = 0.8 - 0.6 * math.exp(-0.3 * l)
        last = l == DEPTH - 1
        w_l = w_in[l].astype(BF16)
        wo_l = w_out[l].astype(BF16)
        mod = mod_all[l]
        ng = norm_g[l].reshape(1, D_MODEL)
        sa = A_DK ** -0.5 * LOG2E
        sb = HEAD_DIM ** -0.5 * LOG2E
        gain = jnp.concatenate([
            jnp.tile(diff_q_norm[l], 8) * sa, jnp.tile(diff_k_norm[l], 8), ones128, ones128, ones128, ones128,
            jnp.tile(gqa_q_norm[l], 6) * sb, jnp.tile(gqa_k_norm[l], 2), ones128, ones128, ones128, ones128,
            jnp.tile(nat_q_norm[l], 6) * sb, jnp.tile(nat_k_norm[l], 6),
            ones128, ones128, ones128, ones128, ones128, ones128]).reshape(1, IN_W)
        lamp = jnp.zeros((8, LANES), F32)
        lamp = lamp.at[0, :A_DK].set(lambda_q1[l]).at[1, :A_DK].set(lambda_k1[l])
        lamp = lamp.at[2, :A_DK].set(lambda_q2[l]).at[3, :A_DK].set(lambda_k2[l])
        sub = jnp.tile(diff_subln[l], 2).reshape(1, LANES)
        bias = _bias_call(nat_rpb[l].reshape(-1), rows)

        lat = _proj_call(xf, mod, ng, w_l, gain, rope_lat, bd32, bd64,
                         nb=nb, seq=seq, tm=512, rope=True, mod_row=None)
        cx = _proj_call(cf, mod, ng, w_l, gain, rope_ctx, bd32, bd64,
                        nb=nb, seq=n_ctx, tm=n_ctx, rope=False, mod_row=nb)
        qa, qb, qc, kta, ktb, ktc, va, vb, vc, ga, gb, gc = lat
        qa_c, qb_c, qc_c, kta_c, ktb_c, ktc_c, va_c, vb_c, vc_c, ga_c, gb_c, gc_c = cx

        ya = _attn_a_call(lamp, sub, qa, ga, kta_c, va_c, kta, va, nb=nb, sq=seq, tq=256, lam_init=lam_init)
        yb = _attn_plain_call(qb, gb, ktb_c, vb_c, ktb, vb, nb=nb, sq=seq, tq=256, heads=_B_HEADS_MAP,
                              name="attn_b_lat")
        yc = _attn_c_call(qc, gc, ktc_c, vc_c, ktc, vc, bias, nb=nb, sq=seq)
        xf_new = _out_call(xf, mod, ya, yb, yc, wo_l, seq=seq, tm=512, mod_row=None, name="out_lat")

        if not last:
            ya_c = _attn_a_call(lamp, sub, qa_c, ga_c, kta_c, va_c, None, None, nb=nb, sq=n_ctx, tq=n_ctx,
                                lam_init=lam_init)
            yb_c = _attn_plain_call(qb_c, gb_c, ktb_c, vb_c, None, None, nb=nb, sq=n_ctx, tq=n_ctx,
                                    heads=_B_HEADS_MAP, name="attn_b_ctx")
            yc_c = _attn_plain_call(qc_c, gc_c, ktc_c, vc_c, None, None, nb=nb, sq=n_ctx, tq=n_ctx,
                                    heads=_C_HEADS_MAP, name="attn_c_ctx")
            cf = _out_call(cf, mod, ya_c, yb_c, yc_c, wo_l, seq=n_ctx, tm=n_ctx, mod_row=nb, name="out_ctx")
        xf = xf_new

    return xf.reshape(nb, seq, D_MODEL)
```

```python
import functools
import math

import numpy as np
import jax
import jax.numpy as jnp
from jax import lax
from jax.experimental import pallas as pl
from jax.experimental.pallas import tpu as pltpu

D_MODEL = 1024
DEPTH = 4
GRID_W = 64
HEAD_DIM = 64
A_HEADS = 4
A_DK = 32
B_HEADS = 6
B_KV_HEADS = 2
C_HEADS = 6
WIN_H = 8
WIN_W = 16
ROPE_THETA = 10000.0
EPS = 1e-6
IN_W = 3584

LANES = 128
MXU_W = 256
LOG2E = 1.4426950408889634
NEG = -1e30
VMEM_LIMIT = 56 * 1024 * 1024

F32 = jnp.float32
BF16 = jnp.bfloat16

_ROLES = (["qa"] * 2 + ["ka"] * 2 + ["va"] * 2 + ["ga"] * 2 + ["qb"] * 3 + ["kb"] + ["vb"] + ["gb"] * 3
          + ["qn"] * 3 + ["kn"] * 3 + ["vn"] * 3 + ["gn"] * 3)
_NORM_D = {"qa": A_DK, "ka": A_DK, "qb": HEAD_DIM, "kb": HEAD_DIM, "qn": HEAD_DIM, "kn": HEAD_DIM}
_ROPE_ROLES = ("qa", "ka", "qb", "kb")


def _dot(a, b):
    return jnp.dot(a, b, preferred_element_type=F32)


def _silu(x):
    return x * (1.0 / (1.0 + jnp.exp(-x)))


def _ada_kernel(c_ref, w_ref, b_ref, o_ref):
    a = _silu(c_ref[...])
    o_ref[0] = jnp.dot(a, w_ref[0], preferred_element_type=F32,
                       precision=lax.Precision.HIGHEST) + b_ref[0]


def _ada_call(cc, w_ada, b_ada):
    n = cc.shape[0]
    nblk = 3 * D_MODEL // D_MODEL
    return pl.pallas_call(
        _ada_kernel,
        grid=(DEPTH, nblk),
        in_specs=[
            pl.BlockSpec((n, D_MODEL), lambda l, j: (0, 0)),
            pl.BlockSpec((1, D_MODEL, D_MODEL), lambda l, j: (l, 0, j)),
            pl.BlockSpec((1, 1, D_MODEL), lambda l, j: (l, 0, j)),
        ],
        out_specs=pl.BlockSpec((1, n, D_MODEL), lambda l, j: (l, 0, j)),
        out_shape=jax.ShapeDtypeStruct((DEPTH, n, 3 * D_MODEL), F32),
        compiler_params=pltpu.CompilerParams(
            dimension_semantics=("arbitrary", "arbitrary"), vmem_limit_bytes=VMEM_LIMIT),
        name="ada",
    )(cc, w_ada, b_ada)


def _proj_kernel(x_ref, mod_ref, ng_ref, w_ref, gain_ref, rope_ref, bd32_ref, bd64_ref,
                 qa_ref, qb_ref, qc_ref, kta_ref, ktb_ref, ktc_ref, va_ref, vb_ref, vc_ref,
                 ga_ref, gb_ref, gc_ref, *, rope, tm):
    x = x_ref[...]
    ms = jnp.mean(x * x, axis=-1, keepdims=True)
    xn = x * lax.rsqrt(ms + EPS) * ng_ref[...]
    h = xn * (1.0 + mod_ref[0, 1:2, :]) + mod_ref[0, 0:1, :]
    hb = h.astype(BF16)

    lane = lax.broadcasted_iota(jnp.int32, (1, LANES), 1)
    row = lax.broadcasted_iota(jnp.int32, (LANES, 1), 0)
    first_half = {A_DK: (lane % A_DK) < (A_DK // 2), HEAD_DIM: (lane % HEAD_DIM) < (HEAD_DIM // 2)}
    bd = {A_DK: bd32_ref, HEAD_DIM: bd64_ref}
    rope_idx = {A_DK: 0, HEAD_DIM: 2}
    count = {}

    n_ch = IN_W // MXU_W
    y_next = _dot(hb, w_ref[:, 0:MXU_W])
    for ch in range(n_ch):
        c0 = ch * MXU_W
        y = y_next
        if ch + 1 < n_ch:
            y_next = _dot(hb, w_ref[:, c0 + MXU_W:c0 + 2 * MXU_W])
        roles = _ROLES[2 * ch:2 * ch + 2]
        d = _NORM_D.get(roles[0]) or _NORM_D.get(roles[1])
        if d is not None:
            ss = _dot((y * y).astype(BF16), bd[d][...])
            yn = y * lax.rsqrt(ss * (1.0 / d) + EPS) * gain_ref[:, c0:c0 + MXU_W]
        for half in range(2):
            role = roles[half]
            j = count.get(role, 0)
            count[role] = j + 1
            sl = slice(half * LANES, (half + 1) * LANES)
            if role in _NORM_D:
                z = yn[:, sl]
                if rope and role in _ROPE_ROLES:
                    dd = _NORM_D[role]
                    zsw = jnp.where(first_half[dd], pltpu.roll(z, LANES - dd // 2, 1),
                                    pltpu.roll(z, dd // 2, 1))
                    z = z * rope_ref[rope_idx[dd]] + zsw * rope_ref[rope_idx[dd] + 1]
            else:
                z = y[:, sl]
            osl = slice(j * LANES, (j + 1) * LANES)
            if role == "qa":
                qa_ref[:, osl] = z.astype(BF16)
            elif role == "qb":
                qb_ref[:, osl] = z.astype(BF16)
            elif role == "qn":
                qc_ref[:, osl] = z.astype(BF16)
            elif role == "ka":
                zt = z.T
                for s in range(LANES // A_DK):
                    kta_ref[0, 4 * j + s] = jnp.where(row // A_DK == s, zt, 0.0).astype(BF16)
            elif role == "kb":
                zt = z.T
                zst = pltpu.roll(z, HEAD_DIM, 1).T
                lo = row < HEAD_DIM
                ktb_ref[0, 0] = jnp.where(lo, zt, 0.0).astype(BF16)
                ktb_ref[0, 1] = jnp.where(lo, 0.0, zst).astype(BF16)
                ktb_ref[0, 2] = jnp.where(lo, zst, 0.0).astype(BF16)
                ktb_ref[0, 3] = jnp.where(lo, 0.0, zt).astype(BF16)
            elif role == "kn":
                zt = z.T
                lo = row < HEAD_DIM
                for cc in range(tm // MXU_W):
                    piece = zt[:, cc * MXU_W:(cc + 1) * MXU_W]
                    ktc_ref[0, 2 * j, cc] = jnp.where(lo, piece, 0.0).astype(BF16)
                    ktc_ref[0, 2 * j + 1, cc] = jnp.where(lo, 0.0, piece).astype(BF16)
            elif role in ("va", "vb", "vn"):
                v_ref = {"va": va_ref, "vb": vb_ref, "vn": vc_ref}[role]
                lo = lane < HEAD_DIM
                v_ref[0, 2 * j] = jnp.where(lo, z, 1.0).astype(BF16)
                v_ref[0, 2 * j + 1] = jnp.where(lo, pltpu.roll(z, HEAD_DIM, 1), 1.0).astype(BF16)
            elif role == "ga":
                ga_ref[:, osl] = _silu(z)
            elif role == "gb":
                gb_ref[:, osl] = _silu(z)
            elif role == "gn":
                gc_ref[:, osl] = _silu(z)


def _proj_call(xf, mod, ng, w, gain, rope_tab, bd32, bd64, *, nb, seq, tm, rope, mod_row):
    tpb = seq // tm
    nch = seq // MXU_W
    cpt = tm // MXU_W
    t_all = nb * seq
    if mod_row is None:
        mod_map = lambda i: (i // tpb, 0, 0)
    else:
        mod_map = lambda i: (mod_row, 0, 0)
    tok = lambda w_: pl.BlockSpec((tm, w_), lambda i: (i, 0))
    const2 = lambda shp: pl.BlockSpec(shp, lambda i: (0, 0))
    in_specs = [
        tok(D_MODEL),
        pl.BlockSpec((1, 3, D_MODEL), mod_map),
        const2((1, D_MODEL)),
        const2((D_MODEL, IN_W)),
        const2((1, IN_W)),
        pl.BlockSpec((4, tm, LANES), lambda i: (0, i % tpb, 0)),
        const2((MXU_W, MXU_W)),
        const2((MXU_W, MXU_W)),
    ]
    out_shape = [
        jax.ShapeDtypeStruct((t_all, 256), BF16),
        jax.ShapeDtypeStruct((t_all, 384), BF16),
        jax.ShapeDtypeStruct((t_all, 384), BF16),
        jax.ShapeDtypeStruct((nb, 8, LANES, seq), BF16),
        jax.ShapeDtypeStruct((nb, 4, LANES, seq), BF16),
        jax.ShapeDtypeStruct((nb, 6, nch, LANES, MXU_W), BF16),
        jax.ShapeDtypeStruct((nb, 4, seq, LANES), BF16),
        jax.ShapeDtypeStruct((nb, 2, seq, LANES), BF16),
        jax.ShapeDtypeStruct((nb, 6, seq, LANES), BF16),
        jax.ShapeDtypeStruct((t_all, 256), F32),
        jax.ShapeDtypeStruct((t_all, 384), F32),
        jax.ShapeDtypeStruct((t_all, 384), F32),
    ]
    out_specs = [
        tok(256), tok(384), tok(384),
        pl.BlockSpec((1, 8, LANES, tm), lambda i: (i // tpb, 0, 0, i % tpb)),
        pl.BlockSpec((1, 4, LANES, tm), lambda i: (i // tpb, 0, 0, i % tpb)),
        pl.BlockSpec((1, 6, cpt, LANES, MXU_W), lambda i: (i // tpb, 0, i % tpb, 0, 0)),
        pl.BlockSpec((1, 4, tm, LANES), lambda i: (i // tpb, 0, i % tpb, 0)),
        pl.BlockSpec((1, 2, tm, LANES), lambda i: (i // tpb, 0, i % tpb, 0)),
        pl.BlockSpec((1, 6, tm, LANES), lambda i: (i // tpb, 0, i % tpb, 0)),
        tok(256), tok(384), tok(384),
    ]
    return pl.pallas_call(
        functools.partial(_proj_kernel, rope=rope, tm=tm),
        grid=(t_all // tm,),
        in_specs=in_specs,
        out_specs=out_specs,
        out_shape=out_shape,
        compiler_params=pltpu.CompilerParams(
            dimension_semantics=("arbitrary",), vmem_limit_bytes=VMEM_LIMIT),
        name="proj_lat" if rope else "proj_ctx",
    )(xf, mod, ng, w, gain, rope_tab, bd32, bd64)


CHUNK = MXU_W
ATTN_CHUNK = 512


def _fold_lanes(x, op):
    acc = x[:, :LANES]
    for i in range(1, x.shape[1] // LANES):
        acc = op(acc, x[:, i * LANES:(i + 1) * LANES])
    return acc


def _run_units(n_units, chunks, score_chunk, value_chunk, s_scr, finish):
    def qk(u, c, m_run):
        k0, w = chunks[c]
        s = score_chunk(u, c)
        s_scr[u % 2, :, k0:k0 + w] = s
        part = _fold_lanes(s, jnp.maximum)
        return part if m_run is None else jnp.maximum(m_run, part)

    def sm(u, c, m, o):
        k0, w = chunks[c]
        p = jnp.exp2(s_scr[u % 2, :, k0:k0 + w] - m)
        d = _dot(p.astype(BF16), value_chunk(u, c))
        return d if o is None else o + d

    m_run = None
    for c in range(len(chunks)):
        m_run = qk(0, c, m_run)
    for u in range(n_units):
        m = jnp.max(m_run, axis=-1, keepdims=True)
        m_run, o = None, None
        for c in range(len(chunks)):
            if u + 1 < n_units:
                m_run = qk(u + 1, c, m_run)
            o = sm(u, c, m, o)
        finish(u, o)


def _normalise(o, half):
    r = pltpu.roll(o, HEAD_DIM, 1)
    return o * (1.0 / r) if half == 0 else r * (1.0 / o)


def _key_chunks(n_lat_keys, n_ctx_keys, width):
    chunks = [(k, width) for k in range(0, n_lat_keys, width)]
    return chunks + [(n_lat_keys, n_ctx_keys)]


def _attn_a_kernel(*refs, has_lat, lam_init, chunk_w):
    if has_lat:
        lamp_ref, sub_ref, q_ref, g_ref, ktc_ref, vc_ref, kt_ref, v_ref, o_ref, s_scr = refs
    else:
        lamp_ref, sub_ref, q_ref, g_ref, ktc_ref, vc_ref, o_ref, s_scr = refs
    n_lat = kt_ref.shape[3] if has_lat else 0
    chunks = _key_chunks(n_lat, ktc_ref.shape[3], chunk_w)
    lp = lamp_ref[...]
    l1 = jnp.sum(lp[0:1] * lp[1:2], axis=-1, keepdims=True)
    l2 = jnp.sum(lp[2:3] * lp[3:4], axis=-1, keepdims=True)
    lam = jnp.exp(l1) - jnp.exp(l2) + lam_init
    lane = lax.broadcasted_iota(jnp.int32, (1, LANES), 1)

    def score_chunk(a, c):
        grp = a // 4
        k0, w = chunks[c]
        ql = q_ref[:, grp * LANES:(grp + 1) * LANES]
        if k0 < n_lat:
            return _dot(ql, kt_ref[0, a, :, k0:k0 + w])
        return _dot(ql, ktc_ref[0, a])

    def value_chunk(a, c):
        k0, w = chunks[c]
        if k0 < n_lat:
            return v_ref[0, a // 2, k0:k0 + w, :]
        return vc_ref[0, a // 2]

    state = {}

    def finish(a, o):
        h, m = divmod(a, 2)
        grp, half = divmod(h, 2)
        o = _normalise(o, half)
        if m == 0:
            state["o0"] = o
            return
        o = jnp.where(lane // HEAD_DIM == half, state["o0"] - lam * o, 0.0)
        msq = jnp.sum(o * o, axis=-1, keepdims=True) * (1.0 / HEAD_DIM)
        yh = o * lax.rsqrt(msq + EPS)
        if half == 0:
            state["yg"] = yh
            return
        yg = (state["yg"] + yh) * sub_ref[...] * (1.0 - lam_init)
        sl = slice(grp * LANES, (grp + 1) * LANES)
        o_ref[:, sl] = (yg * g_ref[:, sl]).astype(BF16)

    _run_units(2 * A_HEADS, chunks, score_chunk, value_chunk, s_scr, finish)


def _attn_plain_kernel(*refs, has_lat, heads, chunk_w):
    if has_lat:
        q_ref, g_ref, ktc_ref, vc_ref, kt_ref, v_ref, o_ref, s_scr = refs
    else:
        q_ref, g_ref, ktc_ref, vc_ref, o_ref, s_scr = refs
    n_lat = kt_ref.shape[3] if has_lat else 0
    chunks = _key_chunks(n_lat, ktc_ref.shape[-1], chunk_w)
    lane = lax.broadcasted_iota(jnp.int32, (1, LANES), 1)
    units = [(grp, half, ki, vi) for grp, pair in enumerate(heads) for half, (ki, vi) in enumerate(pair)]

    def score_chunk(u, c):
        grp, _, ki, _ = units[u]
        k0, w = chunks[c]
        ql = q_ref[:, grp * LANES:(grp + 1) * LANES]
        if k0 < n_lat:
            return _dot(ql, kt_ref[0, ki, :, k0:k0 + w])
        return _dot(ql, ktc_ref[0, ki, 0] if len(ktc_ref.shape) == 5 else ktc_ref[0, ki])

    def value_chunk(u, c):
        vi = units[u][3]
        k0, w = chunks[c]
        if k0 < n_lat:
            return v_ref[0, vi, k0:k0 + w, :]
        return vc_ref[0, vi]

    state = {}

    def finish(u, o):
        grp, half = units[u][:2]
        o = _normalise(o, half)
        if half == 0:
            state["yg"] = o
            return
        yg = jnp.where(lane < HEAD_DIM, state["yg"], o)
        sl = slice(grp * LANES, (grp + 1) * LANES)
        o_ref[:, sl] = (yg * g_ref[:, sl]).astype(BF16)

    _run_units(len(units), chunks, score_chunk, value_chunk, s_scr, finish)


def _attn_c_kernel(q_ref, g_ref, ktc_ref, vc_ref, kt_ref, v_ref, bias_ref, o_ref, s_scr, *, nwin):
    t = pl.program_id(1)
    j0 = jnp.clip(t - 1, 0, kt_ref.shape[2] - nwin)
    k0 = pl.multiple_of(j0 * CHUNK, CHUNK)
    lane = lax.broadcasted_iota(jnp.int32, (1, LANES), 1)
    chunks = [(c * CHUNK, CHUNK) for c in range(nwin + 1)]

    def score_chunk(h, c):
        grp = h // 2
        ql = q_ref[:, grp * LANES:(grp + 1) * LANES]
        if c < nwin:
            return _dot(ql, kt_ref[0, h, j0 + c]) + bias_ref[0, h, :, c * CHUNK:(c + 1) * CHUNK]
        return _dot(ql, ktc_ref[0, h, 0])

    def value_chunk(h, c):
        if c < nwin:
            return v_ref[0, h, pl.ds(k0 + c * CHUNK, CHUNK), :]
        return vc_ref[0, h]

    state = {}

    def finish(h, o):
        grp, half = divmod(h, 2)
        o = _normalise(o, half)
        if half == 0:
            state["yg"] = o
            return
        yg = jnp.where(lane < HEAD_DIM, state["yg"], o)
        sl = slice(grp * LANES, (grp + 1) * LANES)
        o_ref[:, sl] = (yg * g_ref[:, sl]).astype(BF16)

    _run_units(C_HEADS, chunks, score_chunk, value_chunk, s_scr, finish)


def _attn_specs(width, tq, nq_tiles, kt_shape, v_shape, ktc_shape, vc_shape, has_lat):
    tokq = pl.BlockSpec((tq, width), lambda b, t: (b * nq_tiles + t, 0))

    def per_batch(shape):
        nd = len(shape)
        return pl.BlockSpec((1,) + tuple(shape[1:]), lambda b, t: (b,) + (0,) * (nd - 1))

    specs = [tokq, tokq, per_batch(ktc_shape), per_batch(vc_shape)]
    if has_lat:
        specs += [per_batch(kt_shape), per_batch(v_shape)]
    return specs, tokq


def _score_scratch(tq, n_keys):
    return [pltpu.VMEM((2, tq, n_keys), F32)]


def _attn_a_call(lamp, sub, q, g, ktc, vc, kt, v, *, nb, sq, tq, lam_init):
    has_lat = kt is not None
    nqt = sq // tq
    n_keys = ktc.shape[-1] + (kt.shape[-1] if has_lat else 0)
    specs, tokq = _attn_specs(256, tq, nqt, kt.shape if has_lat else None, v.shape if has_lat else None,
                              ktc.shape, vc.shape, has_lat)
    small = [pl.BlockSpec((8, LANES), lambda b, t: (0, 0)), pl.BlockSpec((1, LANES), lambda b, t: (0, 0))]
    args = [lamp, sub, q, g, ktc, vc] + ([kt, v] if has_lat else [])
    return pl.pallas_call(
        functools.partial(_attn_a_kernel, has_lat=has_lat, lam_init=lam_init, chunk_w=ATTN_CHUNK),
        grid=(nb, nqt),
        in_specs=small + specs,
        out_specs=tokq,
        out_shape=jax.ShapeDtypeStruct((nb * sq, 256), BF16),
        scratch_shapes=_score_scratch(tq, n_keys),
        compiler_params=pltpu.CompilerParams(
            dimension_semantics=("arbitrary", "arbitrary"), vmem_limit_bytes=VMEM_LIMIT),
        name="attn_a_lat" if has_lat else "attn_a_ctx",
    )(*args)


def _attn_plain_call(q, g, ktc, vc, kt, v, *, nb, sq, tq, heads, name):
    has_lat = kt is not None
    nqt = sq // tq
    n_keys = ktc.shape[-1] + (kt.shape[-1] if has_lat else 0)
    specs, tokq = _attn_specs(384, tq, nqt, kt.shape if has_lat else None, v.shape if has_lat else None,
                              ktc.shape, vc.shape, has_lat)
    args = [q, g, ktc, vc] + ([kt, v] if has_lat else [])
    return pl.pallas_call(
        functools.partial(_attn_plain_kernel, has_lat=has_lat, heads=heads, chunk_w=ATTN_CHUNK),
        grid=(nb, nqt),
        in_specs=specs,
        out_specs=tokq,
        out_shape=jax.ShapeDtypeStruct((nb * sq, 384), BF16),
        scratch_shapes=_score_scratch(tq, n_keys),
        compiler_params=pltpu.CompilerParams(
            dimension_semantics=("arbitrary", "arbitrary"), vmem_limit_bytes=VMEM_LIMIT),
        name=name,
    )(*args)


_C_TQ = 4 * GRID_W
_C_NWIN = 3


def _attn_c_call(q, g, ktc, vc, kt, v, bias, *, nb, sq):
    nqt = sq // _C_TQ
    specs, tokq = _attn_specs(384, _C_TQ, nqt, kt.shape, v.shape, ktc.shape, vc.shape, True)
    kind = lambda b, t: (jnp.where(t == 0, 0, jnp.where(t == nqt - 1, 2, 1)), 0, 0, 0)
    specs.append(pl.BlockSpec((1, C_HEADS, _C_TQ, _C_NWIN * MXU_W), kind))
    return pl.pallas_call(
        functools.partial(_attn_c_kernel, nwin=_C_NWIN),
        grid=(nb, nqt),
        in_specs=specs,
        out_specs=tokq,
        out_shape=jax.ShapeDtypeStruct((nb * sq, 384), BF16),
        scratch_shapes=_score_scratch(_C_TQ, (_C_NWIN + 1) * CHUNK),
        compiler_params=pltpu.CompilerParams(
            dimension_semantics=("arbitrary", "arbitrary"), vmem_limit_bytes=VMEM_LIMIT),
        name="attn_c_lat",
    )(q, g, ktc, vc, kt, v, bias)


_RPB_H = 2 * WIN_H - 1
_RPB_W = 2 * WIN_W - 1


def _bias_kernel(rpb_ref, o_ref, *, rows):
    base = pl.program_id(0) * (_RPB_H * _RPB_W)
    lane = lax.broadcasted_iota(jnp.int32, (GRID_W, LANES), 1)
    qc = lax.broadcasted_iota(jnp.int32, (GRID_W, LANES), 0)
    kc = lane % GRID_W
    dcol = kc - qc + (WIN_W - 1)
    c0 = jnp.clip(qc - WIN_W // 2, 0, GRID_W - WIN_W)
    ok_c = (kc >= c0) & (kc < c0 + WIN_W)
    lo = lane < GRID_W
    kh = min(WIN_H, rows)
    nqt = rows // 4
    tables = []
    for dr in range(_RPB_H):
        acc = jnp.full((GRID_W, LANES), rpb_ref[base + dr * _RPB_W], F32)
        for dc in range(1, _RPB_W):
            acc = jnp.where(dcol >= dc, rpb_ref[base + dr * _RPB_W + dc], acc)
        tables.append(acc * LOG2E)
    for kind, t in enumerate((0, 1, nqt - 1)):
        ks = min(max(4 * t - 4, 0), rows - 4 * _C_NWIN)
        for i in range(4):
            qr = 4 * t + i
            r0 = min(max(qr - kh // 2, 0), rows - kh)
            for jj in range(_C_NWIN * 2):
                kra = ks + 2 * jj
                ok_a = r0 <= kra < r0 + kh
                ok_b = r0 <= kra + 1 < r0 + kh
                dra = kra - qr + (WIN_H - 1)
                if ok_a and ok_b:
                    piece = jnp.where(ok_c, jnp.where(lo, tables[dra], tables[dra + 1]), NEG)
                elif ok_a:
                    piece = jnp.where(ok_c & lo, tables[dra], NEG)
                elif ok_b:
                    piece = jnp.where(ok_c & jnp.logical_not(lo), tables[dra + 1], NEG)
                else:
                    piece = jnp.full((GRID_W, LANES), NEG, F32)
                o_ref[kind, 0, i * GRID_W:(i + 1) * GRID_W, jj * LANES:(jj + 1) * LANES] = piece


def _bias_call(rpb_flat, rows):
    return pl.pallas_call(
        functools.partial(_bias_kernel, rows=rows),
        grid=(C_HEADS,),
        in_specs=[pl.BlockSpec(memory_space=pltpu.SMEM)],
        out_specs=pl.BlockSpec((3, 1, _C_TQ, _C_NWIN * MXU_W), lambda h: (0, h, 0, 0)),
        out_shape=jax.ShapeDtypeStruct((3, C_HEADS, _C_TQ, _C_NWIN * MXU_W), F32),
        compiler_params=pltpu.CompilerParams(
            dimension_semantics=("arbitrary",), vmem_limit_bytes=VMEM_LIMIT),
        name="nat_bias",
    )(rpb_flat)


def _out_kernel(x_ref, mod_ref, ya_ref, yb_ref, yc_ref, w_ref, o_ref):
    y = jnp.concatenate([ya_ref[...], yb_ref[...], yc_ref[...]], axis=1)
    o_ref[...] = x_ref[...] + mod_ref[0, 2:3, :] * _dot(y, w_ref[...])


def _out_call(xf, mod, ya, yb, yc, w, *, seq, tm, mod_row, name):
    tpb = seq // tm
    if mod_row is None:
        mod_map = lambda i: (i // tpb, 0, 0)
    else:
        mod_map = lambda i: (mod_row, 0, 0)
    tok = lambda w_: pl.BlockSpec((tm, w_), lambda i: (i, 0))
    return pl.pallas_call(
        _out_kernel,
        grid=(xf.shape[0] // tm,),
        in_specs=[tok(D_MODEL), pl.BlockSpec((1, 3, D_MODEL), mod_map), tok(256), tok(384), tok(384),
                  pl.BlockSpec((D_MODEL, D_MODEL), lambda i: (0, 0))],
        out_specs=tok(D_MODEL),
        out_shape=jax.ShapeDtypeStruct(xf.shape, F32),
        compiler_params=pltpu.CompilerParams(
            dimension_semantics=("arbitrary",), vmem_limit_bytes=VMEM_LIMIT),
        name=name,
    )(xf, mod, ya, yb, yc, w)


def _rope_tables(seq):
    t = np.arange(seq)
    row = (t // GRID_W).astype(np.float32)
    col = (t % GRID_W).astype(np.float32)
    out = []
    for d in (A_DK, HEAD_DIM):
        nf = d // 4
        inv = (np.float32(ROPE_THETA) ** (-np.arange(nf, dtype=np.float32) / np.float32(nf))).astype(np.float32)
        ang = np.concatenate([row[:, None] * inv, col[:, None] * inv], axis=-1).astype(np.float32)
        cos = np.cos(ang).astype(np.float32)
        sin = np.sin(ang).astype(np.float32)
        cos_h = np.concatenate([cos, cos], axis=-1)
        sin_h = np.concatenate([-sin, sin], axis=-1)
        out.append(np.tile(cos_h, (1, LANES // d)))
        out.append(np.tile(sin_h, (1, LANES // d)))
    return np.stack(out).astype(np.float32)


def _block_diag(d):
    i = np.arange(MXU_W)
    return (i[:, None] // d == i[None, :] // d).astype(np.float32)


_B_HEADS_MAP = tuple(tuple(((h // 3) * 2 + (h % 2), h // 3) for h in (2 * g, 2 * g + 1)) for g in range(3))
_C_HEADS_MAP = tuple(tuple((h, h) for h in (2 * g, 2 * g + 1)) for g in range(3))


def kernel(x, c, ctx, c_ctx, norm_g, w_ada, b_ada, w_in, w_out, diff_q_norm, diff_k_norm, lambda_q1, lambda_k1,
           lambda_q2, lambda_k2, diff_subln, gqa_q_norm, gqa_k_norm, nat_q_norm, nat_k_norm, nat_rpb):
    nb, seq, _ = x.shape
    n_ctx = ctx.shape[1]
    rows = seq // GRID_W

    cc = jnp.concatenate([c, c_ctx[None, :], jnp.zeros((16 - nb - 1, D_MODEL), F32)], axis=0)
    mod_all = _ada_call(cc, w_ada, b_ada.reshape(DEPTH, 1, 3 * D_MODEL)).reshape(DEPTH, 16, 3, D_MODEL)

    rope_lat = jnp.asarray(_rope_tables(seq))
    rope_ctx = jnp.zeros((4, n_ctx, LANES), F32)
    bd32 = jnp.asarray(_block_diag(A_DK), BF16)
    bd64 = jnp.asarray(_block_diag(HEAD_DIM), BF16)

    xf = x.reshape(nb * seq, D_MODEL)
    cf = ctx.reshape(nb * n_ctx, D_MODEL)
    ones128 = jnp.ones((LANES,), F32)

    for l in range(DEPTH):
        lam_init = 0.8 - 0.6 * math.exp(-0.3 * l)
        last = l == DEPTH - 1
        w_l = w_in[l].astype(BF16)
        wo_l = w_out[l].astype(BF16)
        mod = mod_all[l]
        ng = norm_g[l].reshape(1, D_MODEL)
        sa = A_DK ** -0.5 * LOG2E
        sb = HEAD_DIM ** -0.5 * LOG2E
        gain = jnp.concatenate([
            jnp.tile(diff_q_norm[l], 8) * sa, jnp.tile(diff_k_norm[l], 8), ones128, ones128, ones128, ones128,
            jnp.tile(gqa_q_norm[l], 6) * sb, jnp.tile(gqa_k_norm[l], 2), ones128, ones128, ones128, ones128,
            jnp.tile(nat_q_norm[l], 6) * sb, jnp.tile(nat_k_norm[l], 6),
            ones128, ones128, ones128, ones128, ones128, ones128]).reshape(1, IN_W)
        lamp = jnp.zeros((8, LANES), F32)
        lamp = lamp.at[0, :A_DK].set(lambda_q1[l]).at[1, :A_DK].set(lambda_k1[l])
        lamp = lamp.at[2, :A_DK].set(lambda_q2[l]).at[3, :A_DK].set(lambda_k2[l])
        sub = jnp.tile(diff_subln[l], 2).reshape(1, LANES)
        bias = _bias_call(nat_rpb[l].reshape(-1), rows)

        lat = _proj_call(xf, mod, ng, w_l, gain, rope_lat, bd32, bd64,
                         nb=nb, seq=seq, tm=512, rope=True, mod_row=None)
        cx = _proj_call(cf, mod, ng, w_l, gain, rope_ctx, bd32, bd64,
                        nb=nb, seq=n_ctx, tm=n_ctx, rope=False, mod_row=nb)
        qa, qb, qc, kta, ktb, ktc, va, vb, vc, ga, gb, gc = lat
        qa_c, qb_c, qc_c, kta_c, ktb_c, ktc_c, va_c, vb_c, vc_c, ga_c, gb_c, gc_c = cx

        ya = _attn_a_call(lamp, sub, qa, ga, kta_c, va_c, kta, va, nb=nb, sq=seq, tq=256, lam_init=lam_init)
        yb = _attn_plain_call(qb, gb, ktb_c, vb_c, ktb, vb, nb=nb, sq=seq, tq=256, heads=_B_HEADS_MAP,
                              name="attn_b_lat")
        yc = _attn_c_call(qc, gc, ktc_c, vc_c, ktc, vc, bias, nb=nb, sq=seq)
        xf_new = _out_call(xf, mod, ya, yb, yc, wo_l, seq=seq, tm=512, mod_row=None, name="out_lat")

        if not last:
            ya_c = _attn_a_call(lamp, sub, qa_c, ga_c, kta_c, va_c, None, None, nb=nb, sq=n_ctx, tq=n_ctx,
                                lam_init=lam_init)
            yb_c = _attn_plain_call(qb_c, gb_c, ktb_c, vb_c, None, None, nb=nb, sq=n_ctx, tq=n_ctx,
                                    heads=_B_HEADS_MAP, name="attn_b_ctx")
            yc_c = _attn_plain_call(qc_c, gc_c, ktc_c, vc_c, None, None, nb=nb, sq=n_ctx, tq=n_ctx,
                                    heads=_C_HEADS_MAP, name="attn_c_ctx")
            cf = _out_call(cf, mod, ya_c, yb_c, yc_c, wo_l, seq=n_ctx, tm=n_ctx, mod_row=nb, name="out_ctx")
        xf = xf_new

    return xf.reshape(nb, seq, D_MODEL)
```

```python
import functools
import math

import numpy as np
import jax
import jax.numpy as jnp
from jax import lax
from jax.experimental import pallas as pl
from jax.experimental.pallas import tpu as pltpu

D_MODEL = 1024
DEPTH = 4
GRID_W = 64
HEAD_DIM = 64
A_HEADS = 4
A_DK = 32
B_HEADS = 6
B_KV_HEADS = 2
C_HEADS = 6
WIN_H = 8
WIN_W = 16
ROPE_THETA = 10000.0
EPS = 1e-6
IN_W = 3584

LANES = 128
MXU_W = 256
LOG2E = 1.4426950408889634
NEG = -1e30
VMEM_LIMIT = 56 * 1024 * 1024

F32 = jnp.float32
BF16 = jnp.bfloat16

_ROLES = (["qa"] * 2 + ["ka"] * 2 + ["va"] * 2 + ["ga"] * 2 + ["qb"] * 3 + ["kb"] + ["vb"] + ["gb"] * 3
          + ["qn"] * 3 + ["kn"] * 3 + ["vn"] * 3 + ["gn"] * 3)
_NORM_D = {"qa": A_DK, "ka": A_DK, "qb": HEAD_DIM, "kb": HEAD_DIM, "qn": HEAD_DIM, "kn": HEAD_DIM}
_ROPE_ROLES = ("qa", "ka", "qb", "kb")


def _dot(a, b):
    return jnp.dot(a, b, preferred_element_type=F32)


def _silu(x):
    return x * (1.0 / (1.0 + jnp.exp(-x)))


def _ada_kernel(c_ref, w_ref, b_ref, o_ref):
    a = _silu(c_ref[...])
    o_ref[0] = jnp.dot(a, w_ref[0], preferred_element_type=F32,
                       precision=lax.Precision.HIGHEST) + b_ref[0]


def _ada_call(cc, w_ada, b_ada):
    n = cc.shape[0]
    nblk = 3 * D_MODEL // D_MODEL
    return pl.pallas_call(
        _ada_kernel,
        grid=(DEPTH, nblk),
        in_specs=[
            pl.BlockSpec((n, D_MODEL), lambda l, j: (0, 0)),
            pl.BlockSpec((1, D_MODEL, D_MODEL), lambda l, j: (l, 0, j)),
            pl.BlockSpec((1, 1, D_MODEL), lambda l, j: (l, 0, j)),
        ],
        out_specs=pl.BlockSpec((1, n, D_MODEL), lambda l, j: (l, 0, j)),
        out_shape=jax.ShapeDtypeStruct((DEPTH, n, 3 * D_MODEL), F32),
        compiler_params=pltpu.CompilerParams(
            dimension_semantics=("arbitrary", "arbitrary"), vmem_limit_bytes=VMEM_LIMIT),
        name="ada",
    )(cc, w_ada, b_ada)


def _proj_kernel(x_ref, mod_ref, ng_ref, w_ref, gain_ref, rope_ref, bd32_ref, bd64_ref,
                 qa_ref, qb_ref, qc_ref, kta_ref, ktb_ref, ktc_ref, va_ref, vb_ref, vc_ref,
                 ga_ref, gb_ref, gc_ref, *, rope, tm):
    x = x_ref[...]
    ms = jnp.mean(x * x, axis=-1, keepdims=True)
    xn = x * lax.rsqrt(ms + EPS) * ng_ref[...]
    h = xn * (1.0 + mod_ref[0, 1:2, :]) + mod_ref[0, 0:1, :]
    hb = h.astype(BF16)

    lane = lax.broadcasted_iota(jnp.int32, (1, LANES), 1)
    row = lax.broadcasted_iota(jnp.int32, (LANES, 1), 0)
    first_half = {A_DK: (lane % A_DK) < (A_DK // 2), HEAD_DIM: (lane % HEAD_DIM) < (HEAD_DIM // 2)}
    bd = {A_DK: bd32_ref, HEAD_DIM: bd64_ref}
    rope_idx = {A_DK: 0, HEAD_DIM: 2}
    count = {}

    n_ch = IN_W // MXU_W
    y_next = _dot(hb, w_ref[:, 0:MXU_W])
    for ch in range(n_ch):
        c0 = ch * MXU_W
        y = y_next
        if ch + 1 < n_ch:
            y_next = _dot(hb, w_ref[:, c0 + MXU_W:c0 + 2 * MXU_W])
        roles = _ROLES[2 * ch:2 * ch + 2]
        d = _NORM_D.get(roles[0]) or _NORM_D.get(roles[1])
        if d is not None:
            ss = _dot((y * y).astype(BF16), bd[d][...])
            yn = y * lax.rsqrt(ss * (1.0 / d) + EPS) * gain_ref[:, c0:c0 + MXU_W]
        for half in range(2):
            role = roles[half]
            j = count.get(role, 0)
            count[role] = j + 1
            sl = slice(half * LANES, (half + 1) * LANES)
            if role in _NORM_D:
                z = yn[:, sl]
                if rope and role in _ROPE_ROLES:
                    dd = _NORM_D[role]
                    zsw = jnp.where(first_half[dd], pltpu.roll(z, LANES - dd // 2, 1),
                                    pltpu.roll(z, dd // 2, 1))
                    z = z * rope_ref[rope_idx[dd]] + zsw * rope_ref[rope_idx[dd] + 1]
            else:
                z = y[:, sl]
            osl = slice(j * LANES, (j + 1) * LANES)
            if role == "qa":
                qa_ref[:, osl] = z.astype(BF16)
            elif role == "qb":
                qb_ref[:, osl] = z.astype(BF16)
            elif role == "qn":
                qc_ref[:, osl] = z.astype(BF16)
            elif role == "ka":
                zt = z.T
                for s in range(LANES // A_DK):
                    kta_ref[0, 4 * j + s] = jnp.where(row // A_DK == s, zt, 0.0).astype(BF16)
            elif role == "kb":
                zt = z.T
                zst = pltpu.roll(z, HEAD_DIM, 1).T
                lo = row < HEAD_DIM
                ktb_ref[0, 0] = jnp.where(lo, zt, 0.0).astype(BF16)
                ktb_ref[0, 1] = jnp.where(lo, 0.0, zst).astype(BF16)
                ktb_ref[0, 2] = jnp.where(lo, zst, 0.0).astype(BF16)
                ktb_ref[0, 3] = jnp.where(lo, 0.0, zt).astype(BF16)
            elif role == "kn":
                zt = z.T
                lo = row < HEAD_DIM
                ktc_ref[0, 2 * j] = jnp.where(lo, zt, 0.0).astype(BF16)
                ktc_ref[0, 2 * j + 1] = jnp.where(lo, 0.0, zt).astype(BF16)
            elif role in ("va", "vb", "vn"):
                v_ref = {"va": va_ref, "vb": vb_ref, "vn": vc_ref}[role]
                lo = lane < HEAD_DIM
                v_ref[0, 2 * j] = jnp.where(lo, z, 1.0).astype(BF16)
                v_ref[0, 2 * j + 1] = jnp.where(lo, pltpu.roll(z, HEAD_DIM, 1), 1.0).astype(BF16)
            elif role == "ga":
                ga_ref[:, osl] = _silu(z)
            elif role == "gb":
                gb_ref[:, osl] = _silu(z)
            elif role == "gn":
                gc_ref[:, osl] = _silu(z)


def _proj_call(xf, mod, ng, w, gain, rope_tab, bd32, bd64, *, nb, seq, tm, rope, mod_row):
    tpb = seq // tm
    t_all = nb * seq
    if mod_row is None:
        mod_map = lambda i: (i // tpb, 0, 0)
    else:
        mod_map = lambda i: (mod_row, 0, 0)
    tok = lambda w_: pl.BlockSpec((tm, w_), lambda i: (i, 0))
    const2 = lambda shp: pl.BlockSpec(shp, lambda i: (0, 0))
    in_specs = [
        tok(D_MODEL),
        pl.BlockSpec((1, 3, D_MODEL), mod_map),
        const2((1, D_MODEL)),
        const2((D_MODEL, IN_W)),
        const2((1, IN_W)),
        pl.BlockSpec((4, tm, LANES), lambda i: (0, i % tpb, 0)),
        const2((MXU_W, MXU_W)),
        const2((MXU_W, MXU_W)),
    ]
    out_shape = [
        jax.ShapeDtypeStruct((t_all, 256), BF16),
        jax.ShapeDtypeStruct((t_all, 384), BF16),
        jax.ShapeDtypeStruct((t_all, 384), BF16),
        jax.ShapeDtypeStruct((nb, 8, LANES, seq), BF16),
        jax.ShapeDtypeStruct((nb, 4, LANES, seq), BF16),
        jax.ShapeDtypeStruct((nb, 6, LANES, seq), BF16),
        jax.ShapeDtypeStruct((nb, 4, seq, LANES), BF16),
        jax.ShapeDtypeStruct((nb, 2, seq, LANES), BF16),
        jax.ShapeDtypeStruct((nb, 6, seq, LANES), BF16),
        jax.ShapeDtypeStruct((t_all, 256), F32),
        jax.ShapeDtypeStruct((t_all, 384), F32),
        jax.ShapeDtypeStruct((t_all, 384), F32),
    ]
    out_specs = [
        tok(256), tok(384), tok(384),
        pl.BlockSpec((1, 8, LANES, tm), lambda i: (i // tpb, 0, 0, i % tpb)),
        pl.BlockSpec((1, 4, LANES, tm), lambda i: (i // tpb, 0, 0, i % tpb)),
        pl.BlockSpec((1, 6, LANES, tm), lambda i: (i // tpb, 0, 0, i % tpb)),
        pl.BlockSpec((1, 4, tm, LANES), lambda i: (i // tpb, 0, i % tpb, 0)),
        pl.BlockSpec((1, 2, tm, LANES), lambda i: (i // tpb, 0, i % tpb, 0)),
        pl.BlockSpec((1, 6, tm, LANES), lambda i: (i // tpb, 0, i % tpb, 0)),
        tok(256), tok(384), tok(384),
    ]
    return pl.pallas_call(
        functools.partial(_proj_kernel, rope=rope, tm=tm),
        grid=(t_all // tm,),
        in_specs=in_specs,
        out_specs=out_specs,
        out_shape=out_shape,
        compiler_params=pltpu.CompilerParams(
            dimension_semantics=("arbitrary",), vmem_limit_bytes=VMEM_LIMIT),
        name="proj_lat" if rope else "proj_ctx",
    )(xf, mod, ng, w, gain, rope_tab, bd32, bd64)


CHUNK = MXU_W
ATTN_CHUNK = 512


def _fold_lanes(x, op):
    acc = x[:, :LANES]
    for i in range(1, x.shape[1] // LANES):
        acc = op(acc, x[:, i * LANES:(i + 1) * LANES])
    return acc


def _run_units(n_units, chunks, score_chunk, value_chunk, s_scr, finish):
    def qk(u, c, m_run):
        k0, w = chunks[c]
        s = score_chunk(u, c)
        s_scr[u % 2, :, k0:k0 + w] = s
        part = _fold_lanes(s, jnp.maximum)
        return part if m_run is None else jnp.maximum(m_run, part)

    def sm(u, c, m, o):
        k0, w = chunks[c]
        p = jnp.exp2(s_scr[u % 2, :, k0:k0 + w] - m)
        d = _dot(p.astype(BF16), value_chunk(u, c))
        return d if o is None else o + d

    m_run = None
    for c in range(len(chunks)):
        m_run = qk(0, c, m_run)
    for u in range(n_units):
        m = jnp.max(m_run, axis=-1, keepdims=True)
        m_run, o = None, None
        for c in range(len(chunks)):
            if u + 1 < n_units:
                m_run = qk(u + 1, c, m_run)
            o = sm(u, c, m, o)
        finish(u, o)


def _normalise(o, half):
    r = pltpu.roll(o, HEAD_DIM, 1)
    return o * (1.0 / r) if half == 0 else r * (1.0 / o)


def _key_chunks(n_lat_keys, n_ctx_keys, width):
    chunks = [(k, width) for k in range(0, n_lat_keys, width)]
    return chunks + [(n_lat_keys, n_ctx_keys)]


def _attn_a_kernel(*refs, has_lat, lam_init, chunk_w, tq):
    if has_lat:
        lamp_ref, sub_ref, q_ref, g_ref, ktc_ref, vc_ref, kt_ref, v_ref, o_ref, s_scr = refs
    else:
        lamp_ref, sub_ref, q_ref, g_ref, ktc_ref, vc_ref, o_ref, s_scr = refs
    n_lat = kt_ref.shape[3] if has_lat else 0
    chunks = _key_chunks(n_lat, ktc_ref.shape[3], chunk_w)
    lp = lamp_ref[...]
    l1 = jnp.sum(lp[0:1] * lp[1:2], axis=-1, keepdims=True)
    l2 = jnp.sum(lp[2:3] * lp[3:4], axis=-1, keepdims=True)
    lam = jnp.exp(l1) - jnp.exp(l2) + lam_init
    lane = lax.broadcasted_iota(jnp.int32, (1, LANES), 1)

    n_sub = q_ref.shape[0] // tq

    def score_chunk(u, c):
        r, a = divmod(u, 2 * A_HEADS)
        grp = a // 4
        k0, w = chunks[c]
        ql = q_ref[r * tq:(r + 1) * tq, grp * LANES:(grp + 1) * LANES]
        if k0 < n_lat:
            return _dot(ql, kt_ref[0, a, :, k0:k0 + w])
        return _dot(ql, ktc_ref[0, a])

    def value_chunk(u, c):
        a = u % (2 * A_HEADS)
        k0, w = chunks[c]
        if k0 < n_lat:
            return v_ref[0, a // 2, k0:k0 + w, :]
        return vc_ref[0, a // 2]

    state = {}

    def finish(u, o):
        r, a = divmod(u, 2 * A_HEADS)
        rows = slice(r * tq, (r + 1) * tq)
        h, m = divmod(a, 2)
        grp, half = divmod(h, 2)
        o = _normalise(o, half)
        if m == 0:
            state["o0"] = o
            return
        o = jnp.where(lane // HEAD_DIM == half, state["o0"] - lam * o, 0.0)
        msq = jnp.sum(o * o, axis=-1, keepdims=True) * (1.0 / HEAD_DIM)
        yh = o * lax.rsqrt(msq + EPS)
        if half == 0:
            state["yg"] = yh
            return
        yg = (state["yg"] + yh) * sub_ref[...] * (1.0 - lam_init)
        sl = slice(grp * LANES, (grp + 1) * LANES)
        o_ref[rows, sl] = (yg * g_ref[rows, sl]).astype(BF16)

    _run_units(n_sub * 2 * A_HEADS, chunks, score_chunk, value_chunk, s_scr, finish)


def _attn_plain_kernel(*refs, has_lat, heads, chunk_w, tq):
    if has_lat:
        q_ref, g_ref, ktc_ref, vc_ref, kt_ref, v_ref, o_ref, s_scr = refs
    else:
        q_ref, g_ref, ktc_ref, vc_ref, o_ref, s_scr = refs
    n_lat = kt_ref.shape[3] if has_lat else 0
    chunks = _key_chunks(n_lat, ktc_ref.shape[-1], chunk_w)
    lane = lax.broadcasted_iota(jnp.int32, (1, LANES), 1)
    n_sub = q_ref.shape[0] // tq
    units = [(r, grp, half, ki, vi) for r in range(n_sub)
             for grp, pair in enumerate(heads) for half, (ki, vi) in enumerate(pair)]

    def score_chunk(u, c):
        r, grp, _, ki, _ = units[u]
        k0, w = chunks[c]
        ql = q_ref[r * tq:(r + 1) * tq, grp * LANES:(grp + 1) * LANES]
        if k0 < n_lat:
            return _dot(ql, kt_ref[0, ki, :, k0:k0 + w])
        return _dot(ql, ktc_ref[0, ki])

    def value_chunk(u, c):
        vi = units[u][4]
        k0, w = chunks[c]
        if k0 < n_lat:
            return v_ref[0, vi, k0:k0 + w, :]
        return vc_ref[0, vi]

    state = {}

    def finish(u, o):
        r, grp, half = units[u][:3]
        rows = slice(r * tq, (r + 1) * tq)
        o = _normalise(o, half)
        if half == 0:
            state["yg"] = o
            return
        yg = jnp.where(lane < HEAD_DIM, state["yg"], o)
        sl = slice(grp * LANES, (grp + 1) * LANES)
        o_ref[rows, sl] = (yg * g_ref[rows, sl]).astype(BF16)

    _run_units(len(units), chunks, score_chunk, value_chunk, s_scr, finish)


def _attn_c_kernel(q_ref, g_ref, ktc_ref, vc_ref, kt_ref, v_ref, bias_ref, o_ref, s_scr, *, nwin, n_sub):
    n_tiles = pl.num_programs(1) * n_sub
    lane = lax.broadcasted_iota(jnp.int32, (1, LANES), 1)
    win = nwin * CHUNK
    chunks = [(0, win), (win, ktc_ref.shape[3])]
    tile = [pl.program_id(1) * n_sub + r for r in range(n_sub)]
    j0 = [jnp.clip(t - 1, 0, kt_ref.shape[3] // CHUNK - nwin) for t in tile]
    k0 = [pl.multiple_of(j * CHUNK, CHUNK) for j in j0]
    kind = [jnp.where(t == 0, 0, jnp.where(t == n_tiles - 1, 2, 1)) for t in tile]

    def score_chunk(u, c):
        r, h = divmod(u, C_HEADS)
        grp = h // 2
        ql = q_ref[r * _C_TQ:(r + 1) * _C_TQ, grp * LANES:(grp + 1) * LANES]
        if c == 0:
            return _dot(ql, kt_ref[0, h, :, pl.ds(k0[r], win)]) + bias_ref[kind[r], h]
        return _dot(ql, ktc_ref[0, h])

    def value_chunk(u, c):
        r, h = divmod(u, C_HEADS)
        if c == 0:
            return v_ref[0, h, pl.ds(k0[r], win), :]
        return vc_ref[0, h]

    state = {}

    def finish(u, o):
        r, h = divmod(u, C_HEADS)
        rows = slice(r * _C_TQ, (r + 1) * _C_TQ)
        grp, half = divmod(h, 2)
        o = _normalise(o, half)
        if half == 0:
            state["yg"] = o
            return
        yg = jnp.where(lane < HEAD_DIM, state["yg"], o)
        sl = slice(grp * LANES, (grp + 1) * LANES)
        o_ref[rows, sl] = (yg * g_ref[rows, sl]).astype(BF16)

    _run_units(n_sub * C_HEADS, chunks, score_chunk, value_chunk, s_scr, finish)


def _attn_specs(width, tq, nq_tiles, kt_shape, v_shape, ktc_shape, vc_shape, has_lat):
    tokq = pl.BlockSpec((tq, width), lambda b, t: (b * nq_tiles + t, 0))

    def per_batch(shape):
        nd = len(shape)
        return pl.BlockSpec((1,) + tuple(shape[1:]), lambda b, t: (b,) + (0,) * (nd - 1))

    specs = [tokq, tokq, per_batch(ktc_shape), per_batch(vc_shape)]
    if has_lat:
        specs += [per_batch(kt_shape), per_batch(v_shape)]
    return specs, tokq


def _score_scratch(tq, n_keys):
    return [pltpu.VMEM((2, tq, n_keys), F32)]


def _attn_a_call(lamp, sub, q, g, ktc, vc, kt, v, *, nb, sq, tq, lam_init, n_sub=1):
    has_lat = kt is not None
    nqt = sq // (tq * n_sub)
    n_keys = ktc.shape[-1] + (kt.shape[-1] if has_lat else 0)
    specs, tokq = _attn_specs(256, tq * n_sub, nqt, kt.shape if has_lat else None, v.shape if has_lat else None,
                              ktc.shape, vc.shape, has_lat)
    small = [pl.BlockSpec((8, LANES), lambda b, t: (0, 0)), pl.BlockSpec((1, LANES), lambda b, t: (0, 0))]
    args = [lamp, sub, q, g, ktc, vc] + ([kt, v] if has_lat else [])
    return pl.pallas_call(
        functools.partial(_attn_a_kernel, has_lat=has_lat, lam_init=lam_init, chunk_w=ATTN_CHUNK, tq=tq),
        grid=(nb, nqt),
        in_specs=small + specs,
        out_specs=tokq,
        out_shape=jax.ShapeDtypeStruct((nb * sq, 256), BF16),
        scratch_shapes=_score_scratch(tq, n_keys),
        compiler_params=pltpu.CompilerParams(
            dimension_semantics=("arbitrary", "arbitrary"), vmem_limit_bytes=VMEM_LIMIT),
        name="attn_a_lat" if has_lat else "attn_a_ctx",
    )(*args)


def _attn_plain_call(q, g, ktc, vc, kt, v, *, nb, sq, tq, heads, name, n_sub=1):
    has_lat = kt is not None
    nqt = sq // (tq * n_sub)
    n_keys = ktc.shape[-1] + (kt.shape[-1] if has_lat else 0)
    specs, tokq = _attn_specs(384, tq * n_sub, nqt, kt.shape if has_lat else None, v.shape if has_lat else None,
                              ktc.shape, vc.shape, has_lat)
    args = [q, g, ktc, vc] + ([kt, v] if has_lat else [])
    return pl.pallas_call(
        functools.partial(_attn_plain_kernel, has_lat=has_lat, heads=heads, chunk_w=ATTN_CHUNK, tq=tq),
        grid=(nb, nqt),
        in_specs=specs,
        out_specs=tokq,
        out_shape=jax.ShapeDtypeStruct((nb * sq, 384), BF16),
        scratch_shapes=_score_scratch(tq, n_keys),
        compiler_params=pltpu.CompilerParams(
            dimension_semantics=("arbitrary", "arbitrary"), vmem_limit_bytes=VMEM_LIMIT),
        name=name,
    )(*args)


_C_TQ = 4 * GRID_W
_C_NWIN = 3


def _attn_c_call(q, g, ktc, vc, kt, v, bias, *, nb, sq, n_sub=2):
    nqt = sq // (_C_TQ * n_sub)
    specs, tokq = _attn_specs(384, _C_TQ * n_sub, nqt, kt.shape, v.shape, ktc.shape, vc.shape, True)
    specs.append(pl.BlockSpec(bias.shape, lambda b, t: (0, 0, 0, 0), pipeline_mode=pl.Buffered(1)))
    return pl.pallas_call(
        functools.partial(_attn_c_kernel, nwin=_C_NWIN, n_sub=n_sub),
        grid=(nb, nqt),
        in_specs=specs,
        out_specs=tokq,
        out_shape=jax.ShapeDtypeStruct((nb * sq, 384), BF16),
        scratch_shapes=_score_scratch(_C_TQ, (_C_NWIN + 1) * CHUNK),
        compiler_params=pltpu.CompilerParams(
            dimension_semantics=("arbitrary", "arbitrary"), vmem_limit_bytes=VMEM_LIMIT),
        name="attn_c_lat",
    )(q, g, ktc, vc, kt, v, bias)


_RPB_H = 2 * WIN_H - 1
_RPB_W = 2 * WIN_W - 1


def _bias_kernel(rpb_ref, o_ref, *, rows):
    base = pl.program_id(0) * (_RPB_H * _RPB_W)
    lane = lax.broadcasted_iota(jnp.int32, (GRID_W, LANES), 1)
    qc = lax.broadcasted_iota(jnp.int32, (GRID_W, LANES), 0)
    kc = lane % GRID_W
    dcol = kc - qc + (WIN_W - 1)
    c0 = jnp.clip(qc - WIN_W // 2, 0, GRID_W - WIN_W)
    ok_c = (kc >= c0) & (kc < c0 + WIN_W)
    lo = lane < GRID_W
    kh = min(WIN_H, rows)
    nqt = rows // 4
    tables = []
    for dr in range(_RPB_H):
        acc = jnp.full((GRID_W, LANES), rpb_ref[base + dr * _RPB_W], F32)
        for dc in range(1, _RPB_W):
            acc = jnp.where(dcol >= dc, rpb_ref[base + dr * _RPB_W + dc], acc)
        tables.append(acc * LOG2E)
    for kind, t in enumerate((0, 1, nqt - 1)):
        ks = min(max(4 * t - 4, 0), rows - 4 * _C_NWIN)
        for i in range(4):
            qr = 4 * t + i
            r0 = min(max(qr - kh // 2, 0), rows - kh)
            for jj in range(_C_NWIN * 2):
                kra = ks + 2 * jj
                ok_a = r0 <= kra < r0 + kh
                ok_b = r0 <= kra + 1 < r0 + kh
                dra = kra - qr + (WIN_H - 1)
                if ok_a and ok_b:
                    piece = jnp.where(ok_c, jnp.where(lo, tables[dra], tables[dra + 1]), NEG)
                elif ok_a:
                    piece = jnp.where(ok_c & lo, tables[dra], NEG)
                elif ok_b:
                    piece = jnp.where(ok_c & jnp.logical_not(lo), tables[dra + 1], NEG)
                else:
                    piece = jnp.full((GRID_W, LANES), NEG, F32)
                o_ref[kind, 0, i * GRID_W:(i + 1) * GRID_W, jj * LANES:(jj + 1) * LANES] = piece


def _bias_call(rpb_flat, rows):
    return pl.pallas_call(
        functools.partial(_bias_kernel, rows=rows),
        grid=(C_HEADS,),
        in_specs=[pl.BlockSpec(memory_space=pltpu.SMEM)],
        out_specs=pl.BlockSpec((3, 1, _C_TQ, _C_NWIN * MXU_W), lambda h: (0, h, 0, 0)),
        out_shape=jax.ShapeDtypeStruct((3, C_HEADS, _C_TQ, _C_NWIN * MXU_W), F32),
        compiler_params=pltpu.CompilerParams(
            dimension_semantics=("arbitrary",), vmem_limit_bytes=VMEM_LIMIT),
        name="nat_bias",
    )(rpb_flat)


def _out_kernel(x_ref, mod_ref, ya_ref, yb_ref, yc_ref, w_ref, o_ref):
    y = jnp.concatenate([ya_ref[...], yb_ref[...], yc_ref[...]], axis=1)
    o_ref[...] = x_ref[...] + mod_ref[0, 2:3, :] * _dot(y, w_ref[...])


def _out_call(xf, mod, ya, yb, yc, w, *, seq, tm, mod_row, name):
    tpb = seq // tm
    if mod_row is None:
        mod_map = lambda i: (i // tpb, 0, 0)
    else:
        mod_map = lambda i: (mod_row, 0, 0)
    tok = lambda w_: pl.BlockSpec((tm, w_), lambda i: (i, 0))
    return pl.pallas_call(
        _out_kernel,
        grid=(xf.shape[0] // tm,),
        in_specs=[tok(D_MODEL), pl.BlockSpec((1, 3, D_MODEL), mod_map), tok(256), tok(384), tok(384),
                  pl.BlockSpec((D_MODEL, D_MODEL), lambda i: (0, 0))],
        out_specs=tok(D_MODEL),
        out_shape=jax.ShapeDtypeStruct(xf.shape, F32),
        compiler_params=pltpu.CompilerParams(
            dimension_semantics=("arbitrary",), vmem_limit_bytes=VMEM_LIMIT),
        name=name,
    )(xf, mod, ya, yb, yc, w)


def _rope_tables(seq):
    t = np.arange(seq)
    row = (t // GRID_W).astype(np.float32)
    col = (t % GRID_W).astype(np.float32)
    out = []
    for d in (A_DK, HEAD_DIM):
        nf = d // 4
        inv = (np.float32(ROPE_THETA) ** (-np.arange(nf, dtype=np.float32) / np.float32(nf))).astype(np.float32)
        ang = np.concatenate([row[:, None] * inv, col[:, None] * inv], axis=-1).astype(np.float32)
        cos = np.cos(ang).astype(np.float32)
        sin = np.sin(ang).astype(np.float32)
        cos_h = np.concatenate([cos, cos], axis=-1)
        sin_h = np.concatenate([-sin, sin], axis=-1)
        out.append(np.tile(cos_h, (1, LANES // d)))
        out.append(np.tile(sin_h, (1, LANES // d)))
    return np.stack(out).astype(np.float32)


def _block_diag(d):
    i = np.arange(MXU_W)
    return (i[:, None] // d == i[None, :] // d).astype(np.float32)


_B_HEADS_MAP = tuple(tuple(((h // 3) * 2 + (h % 2), h // 3) for h in (2 * g, 2 * g + 1)) for g in range(3))
_C_HEADS_MAP = tuple(tuple((h, h) for h in (2 * g, 2 * g + 1)) for g in range(3))


def kernel(x, c, ctx, c_ctx, norm_g, w_ada, b_ada, w_in, w_out, diff_q_norm, diff_k_norm, lambda_q1, lambda_k1,
           lambda_q2, lambda_k2, diff_subln, gqa_q_norm, gqa_k_norm, nat_q_norm, nat_k_norm, nat_rpb):
    nb, seq, _ = x.shape
    n_ctx = ctx.shape[1]
    rows = seq // GRID_W

    cc = jnp.concatenate([c, c_ctx[None, :], jnp.zeros((16 - nb - 1, D_MODEL), F32)], axis=0)
    mod_all = _ada_call(cc, w_ada, b_ada.reshape(DEPTH, 1, 3 * D_MODEL)).reshape(DEPTH, 16, 3, D_MODEL)

    rope_lat = jnp.asarray(_rope_tables(seq))
    rope_ctx = jnp.zeros((4, n_ctx, LANES), F32)
    bd32 = jnp.asarray(_block_diag(A_DK), BF16)
    bd64 = jnp.asarray(_block_diag(HEAD_DIM), BF16)

    xf = x.reshape(nb * seq, D_MODEL)
    cf = ctx.reshape(nb * n_ctx, D_MODEL)
    ones128 = jnp.ones((LANES,), F32)

    for l in range(DEPTH):
        lam_init = 0.8 - 0.6 * math.exp(-0.3 * l)
        last = l == DEPTH - 1
        w_l = w_in[l].astype(BF16)
        wo_l = w_out[l].astype(BF16)
        mod = mod_all[l]
        ng = norm_g[l].reshape(1, D_MODEL)
        sa = A_DK ** -0.5 * LOG2E
        sb = HEAD_DIM ** -0.5 * LOG2E
        gain = jnp.concatenate([
            jnp.tile(diff_q_norm[l], 8) * sa, jnp.tile(diff_k_norm[l], 8), ones128, ones128, ones128, ones128,
            jnp.tile(gqa_q_norm[l], 6) * sb, jnp.tile(gqa_k_norm[l], 2), ones128, ones128, ones128, ones128,
            jnp.tile(nat_q_norm[l], 6) * sb, jnp.tile(nat_k_norm[l], 6),
            ones128, ones128, ones128, ones128, ones128, ones128]).reshape(1, IN_W)
        lamp = jnp.zeros((8, LANES), F32)
        lamp = lamp.at[0, :A_DK].set(lambda_q1[l]).at[1, :A_DK].set(lambda_k1[l])
        lamp = lamp.at[2, :A_DK].set(lambda_q2[l]).at[3, :A_DK].set(lambda_k2[l])
        sub = jnp.tile(diff_subln[l], 2).reshape(1, LANES)
        bias = _bias_call(nat_rpb[l].reshape(-1), rows)

        lat = _proj_call(xf, mod, ng, w_l, gain, rope_lat, bd32, bd64,
                         nb=nb, seq=seq, tm=512, rope=True, mod_row=None)
        cx = _proj_call(cf, mod, ng, w_l, gain, rope_ctx, bd32, bd64,
                        nb=nb, seq=n_ctx, tm=n_ctx, rope=False, mod_row=nb)
        qa, qb, qc, kta, ktb, ktc, va, vb, vc, ga, gb, gc = lat
        qa_c, qb_c, qc_c, kta_c, ktb_c, ktc_c, va_c, vb_c, vc_c, ga_c, gb_c, gc_c = cx

        ya = _attn_a_call(lamp, sub, qa, ga, kta_c, va_c, kta, va, nb=nb, sq=seq, tq=256, lam_init=lam_init, n_sub=2)
        yb = _attn_plain_call(qb, gb, ktb_c, vb_c, ktb, vb, nb=nb, sq=seq, tq=256, heads=_B_HEADS_MAP,
                              name="attn_b_lat", n_sub=2)
        yc = _attn_c_call(qc, gc, ktc_c, vc_c, ktc, vc, bias, nb=nb, sq=seq)
        xf_new = _out_call(xf, mod, ya, yb, yc, wo_l, seq=seq, tm=512, mod_row=None, name="out_lat")

        if not last:
            ya_c = _attn_a_call(lamp, sub, qa_c, ga_c, kta_c, va_c, None, None, nb=nb, sq=n_ctx, tq=n_ctx,
                                lam_init=lam_init)
            yb_c = _attn_plain_call(qb_c, gb_c, ktb_c, vb_c, None, None, nb=nb, sq=n_ctx, tq=n_ctx,
                                    heads=_B_HEADS_MAP, name="attn_b_ctx")
            yc_c = _attn_plain_call(qc_c, gc_c, ktc_c, vc_c, None, None, nb=nb, sq=n_ctx, tq=n_ctx,
                                    heads=_C_HEADS_MAP, name="attn_c_ctx")
            cf = _out_call(cf, mod, ya_c, yb_c, yc_c, wo_l, seq=n_ctx, tm=n_ctx, mod_row=nb, name="out_ctx")
        xf = xf_new

    return xf.reshape(nb, seq, D_MODEL)
```

```python
import functools
import math

import numpy as np
import jax
import jax.numpy as jnp
from jax import lax
from jax.experimental import pallas as pl
from jax.experimental.pallas import tpu as pltpu

D_MODEL = 1024
DEPTH = 4
GRID_W = 64
HEAD_DIM = 64
A_HEADS = 4
A_DK = 32
B_HEADS = 6
B_KV_HEADS = 2
C_HEADS = 6
WIN_H = 8
WIN_W = 16
ROPE_THETA = 10000.0
EPS = 1e-6
IN_W = 3584

LANES = 128
MXU_W = 256
LOG2E = 1.4426950408889634
NEG = -1e30
VMEM_LIMIT = 56 * 1024 * 1024

F32 = jnp.float32
BF16 = jnp.bfloat16

_ROLES = (["qa"] * 2 + ["ka"] * 2 + ["va"] * 2 + ["ga"] * 2 + ["qb"] * 3 + ["kb"] + ["vb"] + ["gb"] * 3
          + ["qn"] * 3 + ["kn"] * 3 + ["vn"] * 3 + ["gn"] * 3)
_NORM_D = {"qa": A_DK, "ka": A_DK, "qb": HEAD_DIM, "kb": HEAD_DIM, "qn": HEAD_DIM, "kn": HEAD_DIM}
_ROPE_ROLES = ("qa", "ka", "qb", "kb")


def _dot(a, b):
    return jnp.dot(a, b, preferred_element_type=F32)


def _silu(x):
    return x * (1.0 / (1.0 + jnp.exp(-x)))


def _ada_kernel(c_ref, w_ref, b_ref, o_ref):
    a = _silu(c_ref[...])
    o_ref[0] = jnp.dot(a, w_ref[0], preferred_element_type=F32,
                       precision=lax.Precision.HIGHEST) + b_ref[0]


def _ada_call(cc, w_ada, b_ada):
    n = cc.shape[0]
    nblk = 3 * D_MODEL // D_MODEL
    return pl.pallas_call(
        _ada_kernel,
        grid=(DEPTH, nblk),
        in_specs=[
            pl.BlockSpec((n, D_MODEL), lambda l, j: (0, 0)),
            pl.BlockSpec((1, D_MODEL, D_MODEL), lambda l, j: (l, 0, j)),
            pl.BlockSpec((1, 1, D_MODEL), lambda l, j: (l, 0, j)),
        ],
        out_specs=pl.BlockSpec((1, n, D_MODEL), lambda l, j: (l, 0, j)),
        out_shape=jax.ShapeDtypeStruct((DEPTH, n, 3 * D_MODEL), F32),
        compiler_params=pltpu.CompilerParams(
            dimension_semantics=("arbitrary", "arbitrary"), vmem_limit_bytes=VMEM_LIMIT),
        name="ada",
    )(cc, w_ada, b_ada)


def _proj_kernel(x_ref, mod_ref, ng_ref, w_ref, gain_ref, rope_ref, bd32_ref, bd64_ref,
                 qa_ref, qb_ref, qc_ref, kta_ref, ktb_ref, ktc_ref, va_ref, vb_ref, vc_ref,
                 ga_ref, gb_ref, gc_ref, *, rope, tm):
    x = x_ref[...]
    ms = jnp.mean(x * x, axis=-1, keepdims=True)
    xn = x * lax.rsqrt(ms + EPS) * ng_ref[...]
    h = xn * (1.0 + mod_ref[0, 1:2, :]) + mod_ref[0, 0:1, :]
    hb = h.astype(BF16)

    lane = lax.broadcasted_iota(jnp.int32, (1, LANES), 1)
    row = lax.broadcasted_iota(jnp.int32, (LANES, 1), 0)
    first_half = {A_DK: (lane % A_DK) < (A_DK // 2), HEAD_DIM: (lane % HEAD_DIM) < (HEAD_DIM // 2)}
    bd = {A_DK: bd32_ref, HEAD_DIM: bd64_ref}
    rope_idx = {A_DK: 0, HEAD_DIM: 2}
    count = {}

    n_ch = IN_W // MXU_W
    y_next = _dot(hb, w_ref[:, 0:MXU_W])
    for ch in range(n_ch):
        c0 = ch * MXU_W
        y = y_next
        if ch + 1 < n_ch:
            y_next = _dot(hb, w_ref[:, c0 + MXU_W:c0 + 2 * MXU_W])
        roles = _ROLES[2 * ch:2 * ch + 2]
        d = _NORM_D.get(roles[0]) or _NORM_D.get(roles[1])
        if d is not None:
            ss = _dot((y * y).astype(BF16), bd[d][...])
            yn = y * lax.rsqrt(ss * (1.0 / d) + EPS) * gain_ref[:, c0:c0 + MXU_W]
        for half in range(2):
            role = roles[half]
            j = count.get(role, 0)
            count[role] = j + 1
            sl = slice(half * LANES, (half + 1) * LANES)
            if role in _NORM_D:
                z = yn[:, sl]
                if rope and role in _ROPE_ROLES:
                    dd = _NORM_D[role]
                    zsw = jnp.where(first_half[dd], pltpu.roll(z, LANES - dd // 2, 1),
                                    pltpu.roll(z, dd // 2, 1))
                    z = z * rope_ref[rope_idx[dd]] + zsw * rope_ref[rope_idx[dd] + 1]
            else:
                z = y[:, sl]
            osl = slice(j * LANES, (j + 1) * LANES)
            if role == "qa":
                qa_ref[:, osl] = z.astype(BF16)
            elif role == "qb":
                qb_ref[:, osl] = z.astype(BF16)
            elif role == "qn":
                qc_ref[:, osl] = z.astype(BF16)
            elif role == "ka":
                zt = z.T
                for s in range(LANES // A_DK):
                    kta_ref[0, 4 * j + s] = jnp.where(row // A_DK == s, zt, 0.0).astype(BF16)
            elif role == "kb":
                zt = z.T
                zst = pltpu.roll(z, HEAD_DIM, 1).T
                lo = row < HEAD_DIM
                ktb_ref[0, 0] = jnp.where(lo, zt, 0.0).astype(BF16)
                ktb_ref[0, 1] = jnp.where(lo, 0.0, zst).astype(BF16)
                ktb_ref[0, 2] = jnp.where(lo, zst, 0.0).astype(BF16)
                ktb_ref[0, 3] = jnp.where(lo, 0.0, zt).astype(BF16)
            elif role == "kn":
                zt = z.T
                lo = row < HEAD_DIM
                ktc_ref[0, 2 * j] = jnp.where(lo, zt, 0.0).astype(BF16)
                ktc_ref[0, 2 * j + 1] = jnp.where(lo, 0.0, zt).astype(BF16)
            elif role in ("va", "vb", "vn"):
                v_ref = {"va": va_ref, "vb": vb_ref, "vn": vc_ref}[role]
                lo = lane < HEAD_DIM
                v_ref[0, 2 * j] = jnp.where(lo, z, 1.0).astype(BF16)
                v_ref[0, 2 * j + 1] = jnp.where(lo, pltpu.roll(z, HEAD_DIM, 1), 1.0).astype(BF16)
            elif role == "ga":
                ga_ref[:, osl] = _silu(z)
            elif role == "gb":
                gb_ref[:, osl] = _silu(z)
            elif role == "gn":
                gc_ref[:, osl] = _silu(z)


def _proj_call(xf, mod, ng, w, gain, rope_tab, bd32, bd64, *, nb, seq, tm, rope, mod_row):
    tpb = seq // tm
    t_all = nb * seq
    if mod_row is None:
        mod_map = lambda i: (i // tpb, 0, 0)
    else:
        mod_map = lambda i: (mod_row, 0, 0)
    tok = lambda w_: pl.BlockSpec((tm, w_), lambda i: (i, 0))
    const2 = lambda shp: pl.BlockSpec(shp, lambda i: (0, 0))
    in_specs = [
        tok(D_MODEL),
        pl.BlockSpec((1, 3, D_MODEL), mod_map),
        const2((1, D_MODEL)),
        const2((D_MODEL, IN_W)),
        const2((1, IN_W)),
        pl.BlockSpec((4, tm, LANES), lambda i: (0, i % tpb, 0)),
        const2((MXU_W, MXU_W)),
        const2((MXU_W, MXU_W)),
    ]
    out_shape = [
        jax.ShapeDtypeStruct((t_all, 256), BF16),
        jax.ShapeDtypeStruct((t_all, 384), BF16),
        jax.ShapeDtypeStruct((t_all, 384), BF16),
        jax.ShapeDtypeStruct((nb, 8, LANES, seq), BF16),
        jax.ShapeDtypeStruct((nb, 4, LANES, seq), BF16),
        jax.ShapeDtypeStruct((nb, 6, LANES, seq), BF16),
        jax.ShapeDtypeStruct((nb, 4, seq, LANES), BF16),
        jax.ShapeDtypeStruct((nb, 2, seq, LANES), BF16),
        jax.ShapeDtypeStruct((nb, 6, seq, LANES), BF16),
        jax.ShapeDtypeStruct((t_all, 256), F32),
        jax.ShapeDtypeStruct((t_all, 384), F32),
        jax.ShapeDtypeStruct((t_all, 384), F32),
    ]
    out_specs = [
        tok(256), tok(384), tok(384),
        pl.BlockSpec((1, 8, LANES, tm), lambda i: (i // tpb, 0, 0, i % tpb)),
        pl.BlockSpec((1, 4, LANES, tm), lambda i: (i // tpb, 0, 0, i % tpb)),
        pl.BlockSpec((1, 6, LANES, tm), lambda i: (i // tpb, 0, 0, i % tpb)),
        pl.BlockSpec((1, 4, tm, LANES), lambda i: (i // tpb, 0, i % tpb, 0)),
        pl.BlockSpec((1, 2, tm, LANES), lambda i: (i // tpb, 0, i % tpb, 0)),
        pl.BlockSpec((1, 6, tm, LANES), lambda i: (i // tpb, 0, i % tpb, 0)),
        tok(256), tok(384), tok(384),
    ]
    return pl.pallas_call(
        functools.partial(_proj_kernel, rope=rope, tm=tm),
        grid=(t_all // tm,),
        in_specs=in_specs,
        out_specs=out_specs,
        out_shape=out_shape,
        compiler_params=pltpu.CompilerParams(
            dimension_semantics=("arbitrary",), vmem_limit_bytes=VMEM_LIMIT),
        name="proj_lat" if rope else "proj_ctx",
    )(xf, mod, ng, w, gain, rope_tab, bd32, bd64)


CHUNK = MXU_W
ATTN_CHUNK = 512


def _fold_lanes(x, op):
    acc = x[:, :LANES]
    for i in range(1, x.shape[1] // LANES):
        acc = op(acc, x[:, i * LANES:(i + 1) * LANES])
    return acc


def _run_units(n_units, chunks, score_chunk, value_chunk, s_scr, finish):
    def qk(u, c, m_run):
        k0, w = chunks[c]
        s = score_chunk(u, c)
        s_scr[u % 2, :, k0:k0 + w] = s
        part = _fold_lanes(s, jnp.maximum)
        return part if m_run is None else jnp.maximum(m_run, part)

    def sm(u, c, m, o):
        k0, w = chunks[c]
        p = jnp.exp2(s_scr[u % 2, :, k0:k0 + w] - m)
        d = _dot(p.astype(BF16), value_chunk(u, c))
        return d if o is None else o + d

    m_run = None
    for c in range(len(chunks)):
        m_run = qk(0, c, m_run)
    for u in range(n_units):
        m = jnp.max(m_run, axis=-1, keepdims=True)
        m_run, o = None, None
        for c in range(len(chunks)):
            if u + 1 < n_units:
                m_run = qk(u + 1, c, m_run)
            o = sm(u, c, m, o)
        finish(u, o)


def _normalise(o, half):
    r = pltpu.roll(o, HEAD_DIM, 1)
    return o * (1.0 / r) if half == 0 else r * (1.0 / o)


def _key_chunks(n_lat_keys, n_ctx_keys, width):
    chunks = [(k, width) for k in range(0, n_lat_keys, width)]
    return chunks + [(n_lat_keys, n_ctx_keys)]


def _diff_finisher(lam, lam_init, sub_ref, lane, emit):
    state = {}

    def finish(a, o):
        h, m = divmod(a, 2)
        grp, half = divmod(h, 2)
        o = _normalise(o, half)
        if m == 0:
            state["o0"] = o
            return
        o = jnp.where(lane // HEAD_DIM == half, state["o0"] - lam * o, 0.0)
        msq = jnp.sum(o * o, axis=-1, keepdims=True) * (1.0 / HEAD_DIM)
        yh = o * lax.rsqrt(msq + EPS)
        if half == 0:
            state["yg"] = yh
            return
        emit(grp, (state["yg"] + yh) * sub_ref[...] * (1.0 - lam_init))

    return finish


def _plain_finisher(lane, emit):
    state = {}

    def finish(grp, half, o):
        o = _normalise(o, half)
        if half == 0:
            state["yg"] = o
            return
        emit(grp, jnp.where(lane < HEAD_DIM, state["yg"], o))

    return finish


def _diff_lambda(lamp_ref, lam_init):
    lp = lamp_ref[...]
    l1 = jnp.sum(lp[0:1] * lp[1:2], axis=-1, keepdims=True)
    l2 = jnp.sum(lp[2:3] * lp[3:4], axis=-1, keepdims=True)
    return jnp.exp(l1) - jnp.exp(l2) + lam_init


def _attn_a_kernel(*refs, has_lat, lam_init, chunk_w, tq):
    if has_lat:
        lamp_ref, sub_ref, q_ref, g_ref, ktc_ref, vc_ref, kt_ref, v_ref, o_ref, s_scr = refs
    else:
        lamp_ref, sub_ref, q_ref, g_ref, ktc_ref, vc_ref, o_ref, s_scr = refs
    n_lat = kt_ref.shape[3] if has_lat else 0
    chunks = _key_chunks(n_lat, ktc_ref.shape[3], chunk_w)
    lam = _diff_lambda(lamp_ref, lam_init)
    lane = lax.broadcasted_iota(jnp.int32, (1, LANES), 1)
    n_sub = q_ref.shape[0] // tq

    def score_chunk(u, c):
        r, a = divmod(u, 2 * A_HEADS)
        grp = a // 4
        k0, w = chunks[c]
        ql = q_ref[r * tq:(r + 1) * tq, grp * LANES:(grp + 1) * LANES]
        if k0 < n_lat:
            return _dot(ql, kt_ref[0, a, :, k0:k0 + w])
        return _dot(ql, ktc_ref[0, a])

    def value_chunk(u, c):
        a = u % (2 * A_HEADS)
        k0, w = chunks[c]
        if k0 < n_lat:
            return v_ref[0, a // 2, k0:k0 + w, :]
        return vc_ref[0, a // 2]

    def make_fin(r):
        rows = slice(r * tq, (r + 1) * tq)

        def emit(grp, y):
            sl = slice(grp * LANES, (grp + 1) * LANES)
            o_ref[rows, sl] = (y * g_ref[rows, sl]).astype(BF16)

        return _diff_finisher(lam, lam_init, sub_ref, lane, emit)

    fin = [make_fin(r) for r in range(n_sub)]

    def finish(u, o):
        r, a = divmod(u, 2 * A_HEADS)
        fin[r](a, o)

    _run_units(n_sub * 2 * A_HEADS, chunks, score_chunk, value_chunk, s_scr, finish)


def _attn_plain_kernel(*refs, has_lat, heads, chunk_w, tq):
    if has_lat:
        q_ref, g_ref, ktc_ref, vc_ref, kt_ref, v_ref, o_ref, s_scr = refs
    else:
        q_ref, g_ref, ktc_ref, vc_ref, o_ref, s_scr = refs
    n_lat = kt_ref.shape[3] if has_lat else 0
    chunks = _key_chunks(n_lat, ktc_ref.shape[-1], chunk_w)
    lane = lax.broadcasted_iota(jnp.int32, (1, LANES), 1)
    n_sub = q_ref.shape[0] // tq
    units = [(r, grp, half, ki, vi) for r in range(n_sub)
             for grp, pair in enumerate(heads) for half, (ki, vi) in enumerate(pair)]

    def score_chunk(u, c):
        r, grp, _, ki, _ = units[u]
        k0, w = chunks[c]
        ql = q_ref[r * tq:(r + 1) * tq, grp * LANES:(grp + 1) * LANES]
        if k0 < n_lat:
            return _dot(ql, kt_ref[0, ki, :, k0:k0 + w])
        return _dot(ql, ktc_ref[0, ki])

    def value_chunk(u, c):
        vi = units[u][4]
        k0, w = chunks[c]
        if k0 < n_lat:
            return v_ref[0, vi, k0:k0 + w, :]
        return vc_ref[0, vi]

    def make_fin(r):
        rows = slice(r * tq, (r + 1) * tq)

        def emit(grp, y):
            sl = slice(grp * LANES, (grp + 1) * LANES)
            o_ref[rows, sl] = (y * g_ref[rows, sl]).astype(BF16)

        return _plain_finisher(lane, emit)

    fin = [make_fin(r) for r in range(n_sub)]

    def finish(u, o):
        r, grp, half = units[u][:3]
        fin[r](grp, half, o)

    _run_units(len(units), chunks, score_chunk, value_chunk, s_scr, finish)


def _attn_c_kernel(q_ref, g_ref, ktc_ref, vc_ref, kt_ref, v_ref, bias_ref, o_ref, s_scr, *, nwin, n_sub):
    n_tiles = pl.num_programs(1) * n_sub
    lane = lax.broadcasted_iota(jnp.int32, (1, LANES), 1)
    win = nwin * CHUNK
    chunks = [(0, win), (win, ktc_ref.shape[3])]
    tile = [pl.program_id(1) * n_sub + r for r in range(n_sub)]
    j0 = [jnp.clip(t - 1, 0, kt_ref.shape[3] // CHUNK - nwin) for t in tile]
    k0 = [pl.multiple_of(j * CHUNK, CHUNK) for j in j0]
    kind = [jnp.where(t == 0, 0, jnp.where(t == n_tiles - 1, 2, 1)) for t in tile]

    def score_chunk(u, c):
        r, h = divmod(u, C_HEADS)
        grp = h // 2
        ql = q_ref[r * _C_TQ:(r + 1) * _C_TQ, grp * LANES:(grp + 1) * LANES]
        if c == 0:
            return _dot(ql, kt_ref[0, h, :, pl.ds(k0[r], win)]) + bias_ref[kind[r], h]
        return _dot(ql, ktc_ref[0, h])

    def value_chunk(u, c):
        r, h = divmod(u, C_HEADS)
        if c == 0:
            return v_ref[0, h, pl.ds(k0[r], win), :]
        return vc_ref[0, h]

    def make_fin(r):
        rows = slice(r * _C_TQ, (r + 1) * _C_TQ)

        def emit(grp, y):
            sl = slice(grp * LANES, (grp + 1) * LANES)
            o_ref[rows, sl] = (y * g_ref[rows, sl]).astype(BF16)

        return _plain_finisher(lane, emit)

    fin = [make_fin(r) for r in range(n_sub)]

    def finish(u, o):
        r, h = divmod(u, C_HEADS)
        fin[r](h // 2, h % 2, o)

    _run_units(n_sub * C_HEADS, chunks, score_chunk, value_chunk, s_scr, finish)


def _attn_specs(width, tq, nq_tiles, kt_shape, v_shape, ktc_shape, vc_shape, has_lat):
    tokq = pl.BlockSpec((tq, width), lambda b, t: (b * nq_tiles + t, 0))

    def per_batch(shape):
        nd = len(shape)
        return pl.BlockSpec((1,) + tuple(shape[1:]), lambda b, t: (b,) + (0,) * (nd - 1))

    specs = [tokq, tokq, per_batch(ktc_shape), per_batch(vc_shape)]
    if has_lat:
        specs += [per_batch(kt_shape), per_batch(v_shape)]
    return specs, tokq


def _score_scratch(tq, n_keys):
    return [pltpu.VMEM((2, tq, n_keys), F32)]


def _attn_a_call(lamp, sub, q, g, ktc, vc, kt, v, *, nb, sq, tq, lam_init, n_sub=1):
    has_lat = kt is not None
    nqt = sq // (tq * n_sub)
    n_keys = ktc.shape[-1] + (kt.shape[-1] if has_lat else 0)
    specs, tokq = _attn_specs(256, tq * n_sub, nqt, kt.shape if has_lat else None, v.shape if has_lat else None,
                              ktc.shape, vc.shape, has_lat)
    small = [pl.BlockSpec((8, LANES), lambda b, t: (0, 0)), pl.BlockSpec((1, LANES), lambda b, t: (0, 0))]
    args = [lamp, sub, q, g, ktc, vc] + ([kt, v] if has_lat else [])
    return pl.pallas_call(
        functools.partial(_attn_a_kernel, has_lat=has_lat, lam_init=lam_init, chunk_w=ATTN_CHUNK, tq=tq),
        grid=(nb, nqt),
        in_specs=small + specs,
        out_specs=tokq,
        out_shape=jax.ShapeDtypeStruct((nb * sq, 256), BF16),
        scratch_shapes=_score_scratch(tq, n_keys),
        compiler_params=pltpu.CompilerParams(
            dimension_semantics=("arbitrary", "arbitrary"), vmem_limit_bytes=VMEM_LIMIT),
        name="attn_a_lat" if has_lat else "attn_a_ctx",
    )(*args)


def _attn_plain_call(q, g, ktc, vc, kt, v, *, nb, sq, tq, heads, name, n_sub=1):
    has_lat = kt is not None
    nqt = sq // (tq * n_sub)
    n_keys = ktc.shape[-1] + (kt.shape[-1] if has_lat else 0)
    specs, tokq = _attn_specs(384, tq * n_sub, nqt, kt.shape if has_lat else None, v.shape if has_lat else None,
                              ktc.shape, vc.shape, has_lat)
    args = [q, g, ktc, vc] + ([kt, v] if has_lat else [])
    return pl.pallas_call(
        functools.partial(_attn_plain_kernel, has_lat=has_lat, heads=heads, chunk_w=ATTN_CHUNK, tq=tq),
        grid=(nb, nqt),
        in_specs=specs,
        out_specs=tokq,
        out_shape=jax.ShapeDtypeStruct((nb * sq, 384), BF16),
        scratch_shapes=_score_scratch(tq, n_keys),
        compiler_params=pltpu.CompilerParams(
            dimension_semantics=("arbitrary", "arbitrary"), vmem_limit_bytes=VMEM_LIMIT),
        name=name,
    )(*args)


_C_TQ = 4 * GRID_W
_C_NWIN = 3


def _attn_c_call(q, g, ktc, vc, kt, v, bias, *, nb, sq, n_sub=2):
    nqt = sq // (_C_TQ * n_sub)
    specs, tokq = _attn_specs(384, _C_TQ * n_sub, nqt, kt.shape, v.shape, ktc.shape, vc.shape, True)
    specs.append(pl.BlockSpec(bias.shape, lambda b, t: (0, 0, 0, 0), pipeline_mode=pl.Buffered(1)))
    return pl.pallas_call(
        functools.partial(_attn_c_kernel, nwin=_C_NWIN, n_sub=n_sub),
        grid=(nb, nqt),
        in_specs=specs,
        out_specs=tokq,
        out_shape=jax.ShapeDtypeStruct((nb * sq, 384), BF16),
        scratch_shapes=_score_scratch(_C_TQ, (_C_NWIN + 1) * CHUNK),
        compiler_params=pltpu.CompilerParams(
            dimension_semantics=("arbitrary", "arbitrary"), vmem_limit_bytes=VMEM_LIMIT),
        name="attn_c_lat",
    )(q, g, ktc, vc, kt, v, bias)


def _ctx_kernel(x_ref, mod_ref, lamp_ref, sub_ref, qa_ref, qb_ref, qc_ref, ga_ref, gb_ref, gc_ref,
                kta_ref, ktb_ref, ktc_ref, va_ref, vb_ref, vc_ref, w_ref, o_ref, s_scr, y_scr, *, lam_init):
    lam = _diff_lambda(lamp_ref, lam_init)
    lane = lax.broadcasted_iota(jnp.int32, (1, LANES), 1)
    chunks = [(0, kta_ref.shape[3])]
    b_units = [(grp, half, ki, vi) for grp, pair in enumerate(_B_HEADS_MAP) for half, (ki, vi) in enumerate(pair)]
    n_a, n_b = 2 * A_HEADS, len(b_units)

    def score_chunk(u, c):
        if u < n_a:
            return _dot(qa_ref[:, (u // 4) * LANES:(u // 4 + 1) * LANES], kta_ref[0, u])
        if u < n_a + n_b:
            grp, _, ki, _ = b_units[u - n_a]
            return _dot(qb_ref[:, grp * LANES:(grp + 1) * LANES], ktb_ref[0, ki])
        h = u - n_a - n_b
        return _dot(qc_ref[:, (h // 2) * LANES:(h // 2 + 1) * LANES], ktc_ref[0, h])

    def value_chunk(u, c):
        if u < n_a:
            return va_ref[0, u // 2]
        if u < n_a + n_b:
            return vb_ref[0, b_units[u - n_a][3]]
        return vc_ref[0, u - n_a - n_b]

    def emitter(g_ref, col0):
        def emit(grp, y):
            sl = slice(grp * LANES, (grp + 1) * LANES)
            y_scr[:, col0 + grp * LANES:col0 + (grp + 1) * LANES] = (y * g_ref[:, sl]).astype(BF16)
        return emit

    fin_a = _diff_finisher(lam, lam_init, sub_ref, lane, emitter(ga_ref, 0))
    fin_b = _plain_finisher(lane, emitter(gb_ref, qa_ref.shape[1]))
    fin_c = _plain_finisher(lane, emitter(gc_ref, qa_ref.shape[1] + qb_ref.shape[1]))

    def finish(u, o):
        if u < n_a:
            fin_a(u, o)
        elif u < n_a + n_b:
            grp, half = b_units[u - n_a][:2]
            fin_b(grp, half, o)
        else:
            h = u - n_a - n_b
            fin_c(h // 2, h % 2, o)

    _run_units(n_a + n_b + C_HEADS, chunks, score_chunk, value_chunk, s_scr, finish)
    o_ref[...] = x_ref[...] + mod_ref[0, 2:3, :] * _dot(y_scr[...], w_ref[...])


def _ctx_call(cf, mod, lamp, sub, cx, w, *, nb, n_ctx, mod_row, lam_init):
    qa, qb, qc, kta, ktb, ktc, va, vb, vc, ga, gb, gc = cx
    tok = lambda w_: pl.BlockSpec((n_ctx, w_), lambda b: (b, 0))
    per_b = lambda a: pl.BlockSpec((1,) + tuple(a.shape[1:]), lambda b: (b, 0, 0, 0))
    const = lambda shp: pl.BlockSpec(shp, lambda b: (0,) * len(shp))
    in_specs = [tok(D_MODEL), pl.BlockSpec((1, 3, D_MODEL), lambda b: (mod_row, 0, 0)),
                const((8, LANES)), const((1, LANES)),
                tok(qa.shape[1]), tok(qb.shape[1]), tok(qc.shape[1]),
                tok(ga.shape[1]), tok(gb.shape[1]), tok(gc.shape[1]),
                per_b(kta), per_b(ktb), per_b(ktc), per_b(va), per_b(vb), per_b(vc),
                const((D_MODEL, D_MODEL))]
    return pl.pallas_call(
        functools.partial(_ctx_kernel, lam_init=lam_init),
        grid=(nb,),
        in_specs=in_specs,
        out_specs=tok(D_MODEL),
        out_shape=jax.ShapeDtypeStruct(cf.shape, F32),
        scratch_shapes=[pltpu.VMEM((2, n_ctx, n_ctx), F32), pltpu.VMEM((n_ctx, D_MODEL), BF16)],
        compiler_params=pltpu.CompilerParams(
            dimension_semantics=("arbitrary",), vmem_limit_bytes=VMEM_LIMIT),
        name="ctx_block",
    )(cf, mod, lamp, sub, qa, qb, qc, ga, gb, gc, kta, ktb, ktc, va, vb, vc, w)


_RPB_H = 2 * WIN_H - 1
_RPB_W = 2 * WIN_W - 1


def _bias_kernel(rpb_ref, o_ref, *, rows):
    base = pl.program_id(0) * (_RPB_H * _RPB_W)
    lane = lax.broadcasted_iota(jnp.int32, (GRID_W, LANES), 1)
    qc = lax.broadcasted_iota(jnp.int32, (GRID_W, LANES), 0)
    kc = lane % GRID_W
    dcol = kc - qc + (WIN_W - 1)
    c0 = jnp.clip(qc - WIN_W // 2, 0, GRID_W - WIN_W)
    ok_c = (kc >= c0) & (kc < c0 + WIN_W)
    lo = lane < GRID_W
    kh = min(WIN_H, rows)
    nqt = rows // 4
    tables = []
    for dr in range(_RPB_H):
        acc = jnp.full((GRID_W, LANES), rpb_ref[base + dr * _RPB_W], F32)
        for dc in range(1, _RPB_W):
            acc = jnp.where(dcol >= dc, rpb_ref[base + dr * _RPB_W + dc], acc)
        tables.append(acc * LOG2E)
    for kind, t in enumerate((0, 1, nqt - 1)):
        ks = min(max(4 * t - 4, 0), rows - 4 * _C_NWIN)
        for i in range(4):
            qr = 4 * t + i
            r0 = min(max(qr - kh // 2, 0), rows - kh)
            for jj in range(_C_NWIN * 2):
                kra = ks + 2 * jj
                ok_a = r0 <= kra < r0 + kh
                ok_b = r0 <= kra + 1 < r0 + kh
                dra = kra - qr + (WIN_H - 1)
                if ok_a and ok_b:
                    piece = jnp.where(ok_c, jnp.where(lo, tables[dra], tables[dra + 1]), NEG)
                elif ok_a:
                    piece = jnp.where(ok_c & lo, tables[dra], NEG)
                elif ok_b:
                    piece = jnp.where(ok_c & jnp.logical_not(lo), tables[dra + 1], NEG)
                else:
                    piece = jnp.full((GRID_W, LANES), NEG, F32)
                o_ref[kind, 0, i * GRID_W:(i + 1) * GRID_W, jj * LANES:(jj + 1) * LANES] = piece


def _bias_call(rpb_flat, rows):
    return pl.pallas_call(
        functools.partial(_bias_kernel, rows=rows),
        grid=(C_HEADS,),
        in_specs=[pl.BlockSpec(memory_space=pltpu.SMEM)],
        out_specs=pl.BlockSpec((3, 1, _C_TQ, _C_NWIN * MXU_W), lambda h: (0, h, 0, 0)),
        out_shape=jax.ShapeDtypeStruct((3, C_HEADS, _C_TQ, _C_NWIN * MXU_W), F32),
        compiler_params=pltpu.CompilerParams(
            dimension_semantics=("arbitrary",), vmem_limit_bytes=VMEM_LIMIT),
        name="nat_bias",
    )(rpb_flat)


def _out_kernel(x_ref, mod_ref, ya_ref, yb_ref, yc_ref, w_ref, o_ref):
    y = jnp.concatenate([ya_ref[...], yb_ref[...], yc_ref[...]], axis=1)
    o_ref[...] = x_ref[...] + mod_ref[0, 2:3, :] * _dot(y, w_ref[...])


def _out_call(xf, mod, ya, yb, yc, w, *, seq, tm, mod_row, name):
    tpb = seq // tm
    if mod_row is None:
        mod_map = lambda i: (i // tpb, 0, 0)
    else:
        mod_map = lambda i: (mod_row, 0, 0)
    tok = lambda w_: pl.BlockSpec((tm, w_), lambda i: (i, 0))
    return pl.pallas_call(
        _out_kernel,
        grid=(xf.shape[0] // tm,),
        in_specs=[tok(D_MODEL), pl.BlockSpec((1, 3, D_MODEL), mod_map), tok(256), tok(384), tok(384),
                  pl.BlockSpec((D_MODEL, D_MODEL), lambda i: (0, 0))],
        out_specs=tok(D_MODEL),
        out_shape=jax.ShapeDtypeStruct(xf.shape, F32),
        compiler_params=pltpu.CompilerParams(
            dimension_semantics=("arbitrary",), vmem_limit_bytes=VMEM_LIMIT),
        name=name,
    )(xf, mod, ya, yb, yc, w)


def _rope_tables(seq):
    t = np.arange(seq)
    row = (t // GRID_W).astype(np.float32)
    col = (t % GRID_W).astype(np.float32)
    out = []
    for d in (A_DK, HEAD_DIM):
        nf = d // 4
        inv = (np.float32(ROPE_THETA) ** (-np.arange(nf, dtype=np.float32) / np.float32(nf))).astype(np.float32)
        ang = np.concatenate([row[:, None] * inv, col[:, None] * inv], axis=-1).astype(np.float32)
        cos = np.cos(ang).astype(np.float32)
        sin = np.sin(ang).astype(np.float32)
        cos_h = np.concatenate([cos, cos], axis=-1)
        sin_h = np.concatenate([-sin, sin], axis=-1)
        out.append(np.tile(cos_h, (1, LANES // d)))
        out.append(np.tile(sin_h, (1, LANES // d)))
    return np.stack(out).astype(np.float32)


def _block_diag(d):
    i = np.arange(MXU_W)
    return (i[:, None] // d == i[None, :] // d).astype(np.float32)


_B_HEADS_MAP = tuple(tuple(((h // 3) * 2 + (h % 2), h // 3) for h in (2 * g, 2 * g + 1)) for g in range(3))
_C_HEADS_MAP = tuple(tuple((h, h) for h in (2 * g, 2 * g + 1)) for g in range(3))


def kernel(x, c, ctx, c_ctx, norm_g, w_ada, b_ada, w_in, w_out, diff_q_norm, diff_k_norm, lambda_q1, lambda_k1,
           lambda_q2, lambda_k2, diff_subln, gqa_q_norm, gqa_k_norm, nat_q_norm, nat_k_norm, nat_rpb):
    nb, seq, _ = x.shape
    n_ctx = ctx.shape[1]
    rows = seq // GRID_W

    cc = jnp.concatenate([c, c_ctx[None, :], jnp.zeros((16 - nb - 1, D_MODEL), F32)], axis=0)
    mod_all = _ada_call(cc, w_ada, b_ada.reshape(DEPTH, 1, 3 * D_MODEL)).reshape(DEPTH, 16, 3, D_MODEL)

    rope_lat = jnp.asarray(_rope_tables(seq))
    rope_ctx = jnp.zeros((4, n_ctx, LANES), F32)
    bd32 = jnp.asarray(_block_diag(A_DK), BF16)
    bd64 = jnp.asarray(_block_diag(HEAD_DIM), BF16)

    xf = x.reshape(nb * seq, D_MODEL)
    cf = ctx.reshape(nb * n_ctx, D_MODEL)
    ones128 = jnp.ones((LANES,), F32)

    for l in range(DEPTH):
        lam_init = 0.8 - 0.6 * math.exp(-0.3 * l)
        last = l == DEPTH - 1
        w_l = w_in[l].astype(BF16)
        wo_l = w_out[l].astype(BF16)
        mod = mod_all[l]
        ng = norm_g[l].reshape(1, D_MODEL)
        sa = A_DK ** -0.5 * LOG2E
        sb = HEAD_DIM ** -0.5 * LOG2E
        gain = jnp.concatenate([
            jnp.tile(diff_q_norm[l], 8) * sa, jnp.tile(diff_k_norm[l], 8), ones128, ones128, ones128, ones128,
            jnp.tile(gqa_q_norm[l], 6) * sb, jnp.tile(gqa_k_norm[l], 2), ones128, ones128, ones128, ones128,
            jnp.tile(nat_q_norm[l], 6) * sb, jnp.tile(nat_k_norm[l], 6),
            ones128, ones128, ones128, ones128, ones128, ones128]).reshape(1, IN_W)
        lamp = jnp.zeros((8, LANES), F32)
        lamp = lamp.at[0, :A_DK].set(lambda_q1[l]).at[1, :A_DK].set(lambda_k1[l])
        lamp = lamp.at[2, :A_DK].set(lambda_q2[l]).at[3, :A_DK].set(lambda_k2[l])
        sub = jnp.tile(diff_subln[l], 2).reshape(1, LANES)
        bias = _bias_call(nat_rpb[l].reshape(-1), rows)

        lat = _proj_call(xf, mod, ng, w_l, gain, rope_lat, bd32, bd64,
                         nb=nb, seq=seq, tm=512, rope=True, mod_row=None)
        cx = _proj_call(cf, mod, ng, w_l, gain, rope_ctx, bd32, bd64,
                        nb=nb, seq=n_ctx, tm=n_ctx, rope=False, mod_row=nb)
        qa, qb, qc, kta, ktb, ktc, va, vb, vc, ga, gb, gc = lat
        qa_c, qb_c, qc_c, kta_c, ktb_c, ktc_c, va_c, vb_c, vc_c, ga_c, gb_c, gc_c = cx

        ya = _attn_a_call(lamp, sub, qa, ga, kta_c, va_c, kta, va, nb=nb, sq=seq, tq=256, lam_init=lam_init, n_sub=2)
        yb = _attn_plain_call(qb, gb, ktb_c, vb_c, ktb, vb, nb=nb, sq=seq, tq=256, heads=_B_HEADS_MAP,
                              name="attn_b_lat", n_sub=2)
        yc = _attn_c_call(qc, gc, ktc_c, vc_c, ktc, vc, bias, nb=nb, sq=seq)
        xf_new = _out_call(xf, mod, ya, yb, yc, wo_l, seq=seq, tm=512, mod_row=None, name="out_lat")

        if not last:
            cf = _ctx_call(cf, mod, lamp, sub, cx, wo_l, nb=nb, n_ctx=n_ctx, mod_row=nb, lam_init=lam_init)
        xf = xf_new

    return xf.reshape(nb, seq, D_MODEL)
```

```python
import functools
import math

import numpy as np
import jax
import jax.numpy as jnp
from jax import lax
from jax.experimental import pallas as pl
from jax.experimental.pallas import tpu as pltpu

D_MODEL = 1024
DEPTH = 4
GRID_W = 64
HEAD_DIM = 64
A_HEADS = 4
A_DK = 32
B_HEADS = 6
B_KV_HEADS = 2
C_HEADS = 6
WIN_H = 8
WIN_W = 16
ROPE_THETA = 10000.0
EPS = 1e-6
IN_W = 3584

LANES = 128
MXU_W = 256
LOG2E = 1.4426950408889634
NEG = -1e30
VMEM_LIMIT = 56 * 1024 * 1024

F32 = jnp.float32
BF16 = jnp.bfloat16

_ROLES = (["qa"] * 2 + ["ka"] * 2 + ["va"] * 2 + ["ga"] * 2 + ["qb"] * 3 + ["kb"] + ["vb"] + ["gb"] * 3
          + ["qn"] * 3 + ["kn"] * 3 + ["vn"] * 3 + ["gn"] * 3)
_NORM_D = {"qa": A_DK, "ka": A_DK, "qb": HEAD_DIM, "kb": HEAD_DIM, "qn": HEAD_DIM, "kn": HEAD_DIM}
_ROPE_ROLES = ("qa", "ka", "qb", "kb")


def _dot(a, b):
    return jnp.dot(a, b, preferred_element_type=F32)


def _silu(x):
    return x * (1.0 / (1.0 + jnp.exp(-x)))


def _ada_kernel(c_ref, w_ref, b_ref, o_ref):
    a = _silu(c_ref[...])
    o_ref[0] = jnp.dot(a, w_ref[0], preferred_element_type=F32,
                       precision=lax.Precision.HIGHEST) + b_ref[0]


def _ada_call(cc, w_ada, b_ada):
    n = cc.shape[0]
    nblk = 3 * D_MODEL // D_MODEL
    return pl.pallas_call(
        _ada_kernel,
        grid=(DEPTH, nblk),
        in_specs=[
            pl.BlockSpec((n, D_MODEL), lambda l, j: (0, 0)),
            pl.BlockSpec((1, D_MODEL, D_MODEL), lambda l, j: (l, 0, j)),
            pl.BlockSpec((1, 1, D_MODEL), lambda l, j: (l, 0, j)),
        ],
        out_specs=pl.BlockSpec((1, n, D_MODEL), lambda l, j: (l, 0, j)),
        out_shape=jax.ShapeDtypeStruct((DEPTH, n, 3 * D_MODEL), F32),
        compiler_params=pltpu.CompilerParams(
            dimension_semantics=("arbitrary", "arbitrary"), vmem_limit_bytes=VMEM_LIMIT),
        name="ada",
    )(cc, w_ada, b_ada)


def _proj_kernel(*refs, rope, tm, fuse_out):
    x_ref, mod_ref, ng_ref, w_ref, gain_ref, rope_ref, bd32_ref, bd64_ref = refs[:8]
    refs = refs[8:]
    if fuse_out:
        ya_ref, yb_ref, yc_ref, wo_ref, modp_ref = refs[:5]
        refs = refs[5:]
    (qa_ref, qb_ref, qc_ref, kta_ref, ktb_ref, ktc_ref, va_ref, vb_ref, vc_ref,
     ga_ref, gb_ref, gc_ref) = refs[:12]
    x = x_ref[...]
    if fuse_out:
        yprev = jnp.concatenate([ya_ref[...], yb_ref[...], yc_ref[...]], axis=1)
        x = x + modp_ref[0, 2:3, :] * _dot(yprev, wo_ref[...])
        refs[12][...] = x
    ms = jnp.mean(x * x, axis=-1, keepdims=True)
    xn = x * lax.rsqrt(ms + EPS) * ng_ref[...]
    h = xn * (1.0 + mod_ref[0, 1:2, :]) + mod_ref[0, 0:1, :]
    hb = h.astype(BF16)

    lane = lax.broadcasted_iota(jnp.int32, (1, LANES), 1)
    row = lax.broadcasted_iota(jnp.int32, (LANES, 1), 0)
    first_half = {A_DK: (lane % A_DK) < (A_DK // 2), HEAD_DIM: (lane % HEAD_DIM) < (HEAD_DIM // 2)}
    bd = {A_DK: bd32_ref, HEAD_DIM: bd64_ref}
    rope_idx = {A_DK: 0, HEAD_DIM: 2}
    count = {}

    n_ch = IN_W // MXU_W
    y_next = _dot(hb, w_ref[:, 0:MXU_W])
    for ch in range(n_ch):
        c0 = ch * MXU_W
        y = y_next
        if ch + 1 < n_ch:
            y_next = _dot(hb, w_ref[:, c0 + MXU_W:c0 + 2 * MXU_W])
        roles = _ROLES[2 * ch:2 * ch + 2]
        d = _NORM_D.get(roles[0]) or _NORM_D.get(roles[1])
        if d is not None:
            ss = _dot((y * y).astype(BF16), bd[d][...])
            yn = y * lax.rsqrt(ss * (1.0 / d) + EPS) * gain_ref[:, c0:c0 + MXU_W]
        for half in range(2):
            role = roles[half]
            j = count.get(role, 0)
            count[role] = j + 1
            sl = slice(half * LANES, (half + 1) * LANES)
            if role in _NORM_D:
                z = yn[:, sl]
                if rope and role in _ROPE_ROLES:
                    dd = _NORM_D[role]
                    zsw = jnp.where(first_half[dd], pltpu.roll(z, LANES - dd // 2, 1),
                                    pltpu.roll(z, dd // 2, 1))
                    z = z * rope_ref[rope_idx[dd]] + zsw * rope_ref[rope_idx[dd] + 1]
            else:
                z = y[:, sl]
            osl = slice(j * LANES, (j + 1) * LANES)
            if role == "qa":
                qa_ref[:, osl] = z.astype(BF16)
            elif role == "qb":
                qb_ref[:, osl] = z.astype(BF16)
            elif role == "qn":
                qc_ref[:, osl] = z.astype(BF16)
            elif role == "ka":
                zt = z.T
                for s in range(LANES // A_DK):
                    kta_ref[0, 4 * j + s] = jnp.where(row // A_DK == s, zt, 0.0).astype(BF16)
            elif role == "kb":
                zt = z.T
                zst = pltpu.roll(z, HEAD_DIM, 1).T
                lo = row < HEAD_DIM
                ktb_ref[0, 0] = jnp.where(lo, zt, 0.0).astype(BF16)
                ktb_ref[0, 1] = jnp.where(lo, 0.0, zst).astype(BF16)
                ktb_ref[0, 2] = jnp.where(lo, zst, 0.0).astype(BF16)
                ktb_ref[0, 3] = jnp.where(lo, 0.0, zt).astype(BF16)
            elif role == "kn":
                zt = z.T
                lo = row < HEAD_DIM
                ktc_ref[0, 2 * j] = jnp.where(lo, zt, 0.0).astype(BF16)
                ktc_ref[0, 2 * j + 1] = jnp.where(lo, 0.0, zt).astype(BF16)
            elif role in ("va", "vb", "vn"):
                v_ref = {"va": va_ref, "vb": vb_ref, "vn": vc_ref}[role]
                lo = lane < HEAD_DIM
                v_ref[0, 2 * j] = jnp.where(lo, z, 1.0).astype(BF16)
                v_ref[0, 2 * j + 1] = jnp.where(lo, pltpu.roll(z, HEAD_DIM, 1), 1.0).astype(BF16)
            elif role == "ga":
                ga_ref[:, osl] = _silu(z)
            elif role == "gb":
                gb_ref[:, osl] = _silu(z)
            elif role == "gn":
                gc_ref[:, osl] = _silu(z)


def _proj_call(xf, mod, ng, w, gain, rope_tab, bd32, bd64, *, nb, seq, tm, rope, mod_row, fuse=None):
    tpb = seq // tm
    t_all = nb * seq
    if mod_row is None:
        mod_map = lambda i: (i // tpb, 0, 0)
    else:
        mod_map = lambda i: (mod_row, 0, 0)
    tok = lambda w_: pl.BlockSpec((tm, w_), lambda i: (i, 0))
    const2 = lambda shp: pl.BlockSpec(shp, lambda i: (0, 0))
    in_specs = [
        tok(D_MODEL),
        pl.BlockSpec((1, 3, D_MODEL), mod_map),
        const2((1, D_MODEL)),
        const2((D_MODEL, IN_W)),
        const2((1, IN_W)),
        pl.BlockSpec((4, tm, LANES), lambda i: (0, i % tpb, 0)),
        const2((MXU_W, MXU_W)),
        const2((MXU_W, MXU_W)),
    ]
    out_shape = [
        jax.ShapeDtypeStruct((t_all, 256), BF16),
        jax.ShapeDtypeStruct((t_all, 384), BF16),
        jax.ShapeDtypeStruct((t_all, 384), BF16),
        jax.ShapeDtypeStruct((nb, 8, LANES, seq), BF16),
        jax.ShapeDtypeStruct((nb, 4, LANES, seq), BF16),
        jax.ShapeDtypeStruct((nb, 6, LANES, seq), BF16),
        jax.ShapeDtypeStruct((nb, 4, seq, LANES), BF16),
        jax.ShapeDtypeStruct((nb, 2, seq, LANES), BF16),
        jax.ShapeDtypeStruct((nb, 6, seq, LANES), BF16),
        jax.ShapeDtypeStruct((t_all, 256), F32),
        jax.ShapeDtypeStruct((t_all, 384), F32),
        jax.ShapeDtypeStruct((t_all, 384), F32),
    ]
    out_specs = [
        tok(256), tok(384), tok(384),
        pl.BlockSpec((1, 8, LANES, tm), lambda i: (i // tpb, 0, 0, i % tpb)),
        pl.BlockSpec((1, 4, LANES, tm), lambda i: (i // tpb, 0, 0, i % tpb)),
        pl.BlockSpec((1, 6, LANES, tm), lambda i: (i // tpb, 0, 0, i % tpb)),
        pl.BlockSpec((1, 4, tm, LANES), lambda i: (i // tpb, 0, i % tpb, 0)),
        pl.BlockSpec((1, 2, tm, LANES), lambda i: (i // tpb, 0, i % tpb, 0)),
        pl.BlockSpec((1, 6, tm, LANES), lambda i: (i // tpb, 0, i % tpb, 0)),
        tok(256), tok(384), tok(384),
    ]
    args = [xf, mod, ng, w, gain, rope_tab, bd32, bd64]
    if fuse is not None:
        ya, yb, yc, wo, modp = fuse
        in_specs += [tok(ya.shape[1]), tok(yb.shape[1]), tok(yc.shape[1]), const2(wo.shape),
                     pl.BlockSpec((1, 3, D_MODEL), mod_map)]
        out_shape.append(jax.ShapeDtypeStruct(xf.shape, F32))
        out_specs.append(tok(D_MODEL))
        args += [ya, yb, yc, wo, modp]
    return pl.pallas_call(
        functools.partial(_proj_kernel, rope=rope, tm=tm, fuse_out=fuse is not None),
        grid=(t_all // tm,),
        in_specs=in_specs,
        out_specs=out_specs,
        out_shape=out_shape,
        compiler_params=pltpu.CompilerParams(
            dimension_semantics=("arbitrary",), vmem_limit_bytes=VMEM_LIMIT),
        name="proj_lat" if rope else "proj_ctx",
    )(*args)


CHUNK = MXU_W
ATTN_CHUNK = 512


def _fold_lanes(x, op):
    acc = x[:, :LANES]
    for i in range(1, x.shape[1] // LANES):
        acc = op(acc, x[:, i * LANES:(i + 1) * LANES])
    return acc


def _run_units(n_units, chunks, score_chunk, value_chunk, s_scr, finish):
    def qk(u, c, m_run):
        k0, w = chunks[c]
        s = score_chunk(u, c)
        s_scr[u % 2, :, k0:k0 + w] = s
        part = _fold_lanes(s, jnp.maximum)
        return part if m_run is None else jnp.maximum(m_run, part)

    def sm(u, c, m, o):
        k0, w = chunks[c]
        p = jnp.exp2(s_scr[u % 2, :, k0:k0 + w] - m)
        d = _dot(p.astype(BF16), value_chunk(u, c))
        return d if o is None else o + d

    m_run = None
    for c in range(len(chunks)):
        m_run = qk(0, c, m_run)
    for u in range(n_units):
        m = jnp.max(m_run, axis=-1, keepdims=True)
        m_run, o = None, None
        for c in range(len(chunks)):
            if u + 1 < n_units:
                m_run = qk(u + 1, c, m_run)
            o = sm(u, c, m, o)
        finish(u, o)


def _normalise(o, half):
    r = pltpu.roll(o, HEAD_DIM, 1)
    return o * (1.0 / r) if half == 0 else r * (1.0 / o)


def _key_chunks(n_lat_keys, n_ctx_keys, width):
    chunks = [(k, width) for k in range(0, n_lat_keys, width)]
    return chunks + [(n_lat_keys, n_ctx_keys)]


def _diff_finisher(lam, lam_init, sub_ref, lane, emit):
    state = {}

    def finish(a, o):
        h, m = divmod(a, 2)
        grp, half = divmod(h, 2)
        o = _normalise(o, half)
        if m == 0:
            state["o0"] = o
            return
        o = jnp.where(lane // HEAD_DIM == half, state["o0"] - lam * o, 0.0)
        msq = jnp.sum(o * o, axis=-1, keepdims=True) * (1.0 / HEAD_DIM)
        yh = o * lax.rsqrt(msq + EPS)
        if half == 0:
            state["yg"] = yh
            return
        emit(grp, (state["yg"] + yh) * sub_ref[...] * (1.0 - lam_init))

    return finish


def _plain_finisher(lane, emit):
    state = {}

    def finish(grp, half, o):
        o = _normalise(o, half)
        if half == 0:
            state["yg"] = o
            return
        emit(grp, jnp.where(lane < HEAD_DIM, state["yg"], o))

    return finish


def _diff_lambda(lamp_ref, lam_init):
    lp = lamp_ref[...]
    l1 = jnp.sum(lp[0:1] * lp[1:2], axis=-1, keepdims=True)
    l2 = jnp.sum(lp[2:3] * lp[3:4], axis=-1, keepdims=True)
    return jnp.exp(l1) - jnp.exp(l2) + lam_init


def _attn_a_kernel(*refs, has_lat, lam_init, chunk_w, tq):
    if has_lat:
        lamp_ref, sub_ref, q_ref, g_ref, ktc_ref, vc_ref, kt_ref, v_ref, o_ref, s_scr = refs
    else:
        lamp_ref, sub_ref, q_ref, g_ref, ktc_ref, vc_ref, o_ref, s_scr = refs
    n_lat = kt_ref.shape[3] if has_lat else 0
    chunks = _key_chunks(n_lat, ktc_ref.shape[3], chunk_w)
    lam = _diff_lambda(lamp_ref, lam_init)
    lane = lax.broadcasted_iota(jnp.int32, (1, LANES), 1)
    n_sub = q_ref.shape[0] // tq

    def score_chunk(u, c):
        r, a = divmod(u, 2 * A_HEADS)
        grp = a // 4
        k0, w = chunks[c]
        ql = q_ref[r * tq:(r + 1) * tq, grp * LANES:(grp + 1) * LANES]
        if k0 < n_lat:
            return _dot(ql, kt_ref[0, a, :, k0:k0 + w])
        return _dot(ql, ktc_ref[0, a])

    def value_chunk(u, c):
        a = u % (2 * A_HEADS)
        k0, w = chunks[c]
        if k0 < n_lat:
            return v_ref[0, a // 2, k0:k0 + w, :]
        return vc_ref[0, a // 2]

    def make_fin(r):
        rows = slice(r * tq, (r + 1) * tq)

        def emit(grp, y):
            sl = slice(grp * LANES, (grp + 1) * LANES)
            o_ref[rows, sl] = (y * g_ref[rows, sl]).astype(BF16)

        return _diff_finisher(lam, lam_init, sub_ref, lane, emit)

    fin = [make_fin(r) for r in range(n_sub)]

    def finish(u, o):
        r, a = divmod(u, 2 * A_HEADS)
        fin[r](a, o)

    _run_units(n_sub * 2 * A_HEADS, chunks, score_chunk, value_chunk, s_scr, finish)


def _attn_plain_kernel(*refs, has_lat, heads, chunk_w, tq):
    if has_lat:
        q_ref, g_ref, ktc_ref, vc_ref, kt_ref, v_ref, o_ref, s_scr = refs
    else:
        q_ref, g_ref, ktc_ref, vc_ref, o_ref, s_scr = refs
    n_lat = kt_ref.shape[3] if has_lat else 0
    chunks = _key_chunks(n_lat, ktc_ref.shape[-1], chunk_w)
    lane = lax.broadcasted_iota(jnp.int32, (1, LANES), 1)
    n_sub = q_ref.shape[0] // tq
    units = [(r, grp, half, ki, vi) for r in range(n_sub)
             for grp, pair in enumerate(heads) for half, (ki, vi) in enumerate(pair)]

    def score_chunk(u, c):
        r, grp, _, ki, _ = units[u]
        k0, w = chunks[c]
        ql = q_ref[r * tq:(r + 1) * tq, grp * LANES:(grp + 1) * LANES]
        if k0 < n_lat:
            return _dot(ql, kt_ref[0, ki, :, k0:k0 + w])
        return _dot(ql, ktc_ref[0, ki])

    def value_chunk(u, c):
        vi = units[u][4]
        k0, w = chunks[c]
        if k0 < n_lat:
            return v_ref[0, vi, k0:k0 + w, :]
        return vc_ref[0, vi]

    def make_fin(r):
        rows = slice(r * tq, (r + 1) * tq)

        def emit(grp, y):
            sl = slice(grp * LANES, (grp + 1) * LANES)
            o_ref[rows, sl] = (y * g_ref[rows, sl]).astype(BF16)

        return _plain_finisher(lane, emit)

    fin = [make_fin(r) for r in range(n_sub)]

    def finish(u, o):
        r, grp, half = units[u][:3]
        fin[r](grp, half, o)

    _run_units(len(units), chunks, score_chunk, value_chunk, s_scr, finish)


def _attn_c_kernel(q_ref, g_ref, ktc_ref, vc_ref, kt_ref, v_ref, bias_ref, o_ref, s_scr, *, nwin, n_sub):
    n_tiles = pl.num_programs(1) * n_sub
    lane = lax.broadcasted_iota(jnp.int32, (1, LANES), 1)
    win = nwin * CHUNK
    chunks = [(0, win), (win, ktc_ref.shape[3])]
    tile = [pl.program_id(1) * n_sub + r for r in range(n_sub)]
    j0 = [jnp.clip(t - 1, 0, kt_ref.shape[3] // CHUNK - nwin) for t in tile]
    k0 = [pl.multiple_of(j * CHUNK, CHUNK) for j in j0]
    kind = [jnp.where(t == 0, 0, jnp.where(t == n_tiles - 1, 2, 1)) for t in tile]

    def score_chunk(u, c):
        r, h = divmod(u, C_HEADS)
        grp = h // 2
        ql = q_ref[r * _C_TQ:(r + 1) * _C_TQ, grp * LANES:(grp + 1) * LANES]
        if c == 0:
            return _dot(ql, kt_ref[0, h, :, pl.ds(k0[r], win)]) + bias_ref[kind[r], h]
        return _dot(ql, ktc_ref[0, h])

    def value_chunk(u, c):
        r, h = divmod(u, C_HEADS)
        if c == 0:
            return v_ref[0, h, pl.ds(k0[r], win), :]
        return vc_ref[0, h]

    def make_fin(r):
        rows = slice(r * _C_TQ, (r + 1) * _C_TQ)

        def emit(grp, y):
            sl = slice(grp * LANES, (grp + 1) * LANES)
            o_ref[rows, sl] = (y * g_ref[rows, sl]).astype(BF16)

        return _plain_finisher(lane, emit)

    fin = [make_fin(r) for r in range(n_sub)]

    def finish(u, o):
        r, h = divmod(u, C_HEADS)
        fin[r](h // 2, h % 2, o)

    _run_units(n_sub * C_HEADS, chunks, score_chunk, value_chunk, s_scr, finish)


def _attn_specs(width, tq, nq_tiles, kt_shape, v_shape, ktc_shape, vc_shape, has_lat):
    tokq = pl.BlockSpec((tq, width), lambda b, t: (b * nq_tiles + t, 0))

    def per_batch(shape):
        nd = len(shape)
        return pl.BlockSpec((1,) + tuple(shape[1:]), lambda b, t: (b,) + (0,) * (nd - 1))

    specs = [tokq, tokq, per_batch(ktc_shape), per_batch(vc_shape)]
    if has_lat:
        specs += [per_batch(kt_shape), per_batch(v_shape)]
    return specs, tokq


def _score_scratch(tq, n_keys):
    return [pltpu.VMEM((2, tq, n_keys), F32)]


def _attn_a_call(lamp, sub, q, g, ktc, vc, kt, v, *, nb, sq, tq, lam_init, n_sub=1):
    has_lat = kt is not None
    nqt = sq // (tq * n_sub)
    n_keys = ktc.shape[-1] + (kt.shape[-1] if has_lat else 0)
    specs, tokq = _attn_specs(256, tq * n_sub, nqt, kt.shape if has_lat else None, v.shape if has_lat else None,
                              ktc.shape, vc.shape, has_lat)
    small = [pl.BlockSpec((8, LANES), lambda b, t: (0, 0)), pl.BlockSpec((1, LANES), lambda b, t: (0, 0))]
    args = [lamp, sub, q, g, ktc, vc] + ([kt, v] if has_lat else [])
    return pl.pallas_call(
        functools.partial(_attn_a_kernel, has_lat=has_lat, lam_init=lam_init, chunk_w=ATTN_CHUNK, tq=tq),
        grid=(nb, nqt),
        in_specs=small + specs,
        out_specs=tokq,
        out_shape=jax.ShapeDtypeStruct((nb * sq, 256), BF16),
        scratch_shapes=_score_scratch(tq, n_keys),
        compiler_params=pltpu.CompilerParams(
            dimension_semantics=("arbitrary", "arbitrary"), vmem_limit_bytes=VMEM_LIMIT),
        name="attn_a_lat" if has_lat else "attn_a_ctx",
    )(*args)


def _attn_plain_call(q, g, ktc, vc, kt, v, *, nb, sq, tq, heads, name, n_sub=1):
    has_lat = kt is not None
    nqt = sq // (tq * n_sub)
    n_keys = ktc.shape[-1] + (kt.shape[-1] if has_lat else 0)
    specs, tokq = _attn_specs(384, tq * n_sub, nqt, kt.shape if has_lat else None, v.shape if has_lat else None,
                              ktc.shape, vc.shape, has_lat)
    args = [q, g, ktc, vc] + ([kt, v] if has_lat else [])
    return pl.pallas_call(
        functools.partial(_attn_plain_kernel, has_lat=has_lat, heads=heads, chunk_w=ATTN_CHUNK, tq=tq),
        grid=(nb, nqt),
        in_specs=specs,
        out_specs=tokq,
        out_shape=jax.ShapeDtypeStruct((nb * sq, 384), BF16),
        scratch_shapes=_score_scratch(tq, n_keys),
        compiler_params=pltpu.CompilerParams(
            dimension_semantics=("arbitrary", "arbitrary"), vmem_limit_bytes=VMEM_LIMIT),
        name=name,
    )(*args)


_C_TQ = 4 * GRID_W
_C_NWIN = 3


def _attn_c_call(q, g, ktc, vc, kt, v, bias, *, nb, sq, n_sub=2):
    nqt = sq // (_C_TQ * n_sub)
    specs, tokq = _attn_specs(384, _C_TQ * n_sub, nqt, kt.shape, v.shape, ktc.shape, vc.shape, True)
    specs.append(pl.BlockSpec(bias.shape, lambda b, t: (0, 0, 0, 0), pipeline_mode=pl.Buffered(1)))
    return pl.pallas_call(
        functools.partial(_attn_c_kernel, nwin=_C_NWIN, n_sub=n_sub),
        grid=(nb, nqt),
        in_specs=specs,
        out_specs=tokq,
        out_shape=jax.ShapeDtypeStruct((nb * sq, 384), BF16),
        scratch_shapes=_score_scratch(_C_TQ, (_C_NWIN + 1) * CHUNK),
        compiler_params=pltpu.CompilerParams(
            dimension_semantics=("arbitrary", "arbitrary"), vmem_limit_bytes=VMEM_LIMIT),
        name="attn_c_lat",
    )(q, g, ktc, vc, kt, v, bias)


def _ctx_kernel(x_ref, mod_ref, lamp_ref, sub_ref, qa_ref, qb_ref, qc_ref, ga_ref, gb_ref, gc_ref,
                kta_ref, ktb_ref, ktc_ref, va_ref, vb_ref, vc_ref, w_ref, o_ref, s_scr, y_scr, *, lam_init):
    lam = _diff_lambda(lamp_ref, lam_init)
    lane = lax.broadcasted_iota(jnp.int32, (1, LANES), 1)
    chunks = [(0, kta_ref.shape[3])]
    b_units = [(grp, half, ki, vi) for grp, pair in enumerate(_B_HEADS_MAP) for half, (ki, vi) in enumerate(pair)]
    n_a, n_b = 2 * A_HEADS, len(b_units)

    def score_chunk(u, c):
        if u < n_a:
            return _dot(qa_ref[:, (u // 4) * LANES:(u // 4 + 1) * LANES], kta_ref[0, u])
        if u < n_a + n_b:
            grp, _, ki, _ = b_units[u - n_a]
            return _dot(qb_ref[:, grp * LANES:(grp + 1) * LANES], ktb_ref[0, ki])
        h = u - n_a - n_b
        return _dot(qc_ref[:, (h // 2) * LANES:(h // 2 + 1) * LANES], ktc_ref[0, h])

    def value_chunk(u, c):
        if u < n_a:
            return va_ref[0, u // 2]
        if u < n_a + n_b:
            return vb_ref[0, b_units[u - n_a][3]]
        return vc_ref[0, u - n_a - n_b]

    def emitter(g_ref, col0):
        def emit(grp, y):
            sl = slice(grp * LANES, (grp + 1) * LANES)
            y_scr[:, col0 + grp * LANES:col0 + (grp + 1) * LANES] = (y * g_ref[:, sl]).astype(BF16)
        return emit

    fin_a = _diff_finisher(lam, lam_init, sub_ref, lane, emitter(ga_ref, 0))
    fin_b = _plain_finisher(lane, emitter(gb_ref, qa_ref.shape[1]))
    fin_c = _plain_finisher(lane, emitter(gc_ref, qa_ref.shape[1] + qb_ref.shape[1]))

    def finish(u, o):
        if u < n_a:
            fin_a(u, o)
        elif u < n_a + n_b:
            grp, half = b_units[u - n_a][:2]
            fin_b(grp, half, o)
        else:
            h = u - n_a - n_b
            fin_c(h // 2, h % 2, o)

    _run_units(n_a + n_b + C_HEADS, chunks, score_chunk, value_chunk, s_scr, finish)
    o_ref[...] = x_ref[...] + mod_ref[0, 2:3, :] * _dot(y_scr[...], w_ref[...])


def _ctx_call(cf, mod, lamp, sub, cx, w, *, nb, n_ctx, mod_row, lam_init):
    qa, qb, qc, kta, ktb, ktc, va, vb, vc, ga, gb, gc = cx
    tok = lambda w_: pl.BlockSpec((n_ctx, w_), lambda b: (b, 0))
    per_b = lambda a: pl.BlockSpec((1,) + tuple(a.shape[1:]), lambda b: (b, 0, 0, 0))
    const = lambda shp: pl.BlockSpec(shp, lambda b: (0,) * len(shp))
    in_specs = [tok(D_MODEL), pl.BlockSpec((1, 3, D_MODEL), lambda b: (mod_row, 0, 0)),
                const((8, LANES)), const((1, LANES)),
                tok(qa.shape[1]), tok(qb.shape[1]), tok(qc.shape[1]),
                tok(ga.shape[1]), tok(gb.shape[1]), tok(gc.shape[1]),
                per_b(kta), per_b(ktb), per_b(ktc), per_b(va), per_b(vb), per_b(vc),
                const((D_MODEL, D_MODEL))]
    return pl.pallas_call(
        functools.partial(_ctx_kernel, lam_init=lam_init),
        grid=(nb,),
        in_specs=in_specs,
        out_specs=tok(D_MODEL),
        out_shape=jax.ShapeDtypeStruct(cf.shape, F32),
        scratch_shapes=[pltpu.VMEM((2, n_ctx, n_ctx), F32), pltpu.VMEM((n_ctx, D_MODEL), BF16)],
        compiler_params=pltpu.CompilerParams(
            dimension_semantics=("arbitrary",), vmem_limit_bytes=VMEM_LIMIT),
        name="ctx_block",
    )(cf, mod, lamp, sub, qa, qb, qc, ga, gb, gc, kta, ktb, ktc, va, vb, vc, w)


_RPB_H = 2 * WIN_H - 1
_RPB_W = 2 * WIN_W - 1


def _bias_kernel(rpb_ref, o_ref, *, rows):
    base = pl.program_id(0) * (_RPB_H * _RPB_W)
    lane = lax.broadcasted_iota(jnp.int32, (GRID_W, LANES), 1)
    qc = lax.broadcasted_iota(jnp.int32, (GRID_W, LANES), 0)
    kc = lane % GRID_W
    dcol = kc - qc + (WIN_W - 1)
    c0 = jnp.clip(qc - WIN_W // 2, 0, GRID_W - WIN_W)
    ok_c = (kc >= c0) & (kc < c0 + WIN_W)
    lo = lane < GRID_W
    kh = min(WIN_H, rows)
    nqt = rows // 4
    tables = []
    for dr in range(_RPB_H):
        acc = jnp.full((GRID_W, LANES), rpb_ref[base + dr * _RPB_W], F32)
        for dc in range(1, _RPB_W):
            acc = jnp.where(dcol >= dc, rpb_ref[base + dr * _RPB_W + dc], acc)
        tables.append(acc * LOG2E)
    for kind, t in enumerate((0, 1, nqt - 1)):
        ks = min(max(4 * t - 4, 0), rows - 4 * _C_NWIN)
        for i in range(4):
            qr = 4 * t + i
            r0 = min(max(qr - kh // 2, 0), rows - kh)
            for jj in range(_C_NWIN * 2):
                kra = ks + 2 * jj
                ok_a = r0 <= kra < r0 + kh
                ok_b = r0 <= kra + 1 < r0 + kh
                dra = kra - qr + (WIN_H - 1)
                if ok_a and ok_b:
                    piece = jnp.where(ok_c, jnp.where(lo, tables[dra], tables[dra + 1]), NEG)
                elif ok_a:
                    piece = jnp.where(ok_c & lo, tables[dra], NEG)
                elif ok_b:
                    piece = jnp.where(ok_c & jnp.logical_not(lo), tables[dra + 1], NEG)
                else:
                    piece = jnp.full((GRID_W, LANES), NEG, F32)
                o_ref[kind, 0, i * GRID_W:(i + 1) * GRID_W, jj * LANES:(jj + 1) * LANES] = piece


def _bias_call(rpb_flat, rows):
    return pl.pallas_call(
        functools.partial(_bias_kernel, rows=rows),
        grid=(C_HEADS,),
        in_specs=[pl.BlockSpec(memory_space=pltpu.SMEM)],
        out_specs=pl.BlockSpec((3, 1, _C_TQ, _C_NWIN * MXU_W), lambda h: (0, h, 0, 0)),
        out_shape=jax.ShapeDtypeStruct((3, C_HEADS, _C_TQ, _C_NWIN * MXU_W), F32),
        compiler_params=pltpu.CompilerParams(
            dimension_semantics=("arbitrary",), vmem_limit_bytes=VMEM_LIMIT),
        name="nat_bias",
    )(rpb_flat)


def _out_kernel(x_ref, mod_ref, ya_ref, yb_ref, yc_ref, w_ref, o_ref):
    y = jnp.concatenate([ya_ref[...], yb_ref[...], yc_ref[...]], axis=1)
    o_ref[...] = x_ref[...] + mod_ref[0, 2:3, :] * _dot(y, w_ref[...])


def _out_call(xf, mod, ya, yb, yc, w, *, seq, tm, mod_row, name):
    tpb = seq // tm
    if mod_row is None:
        mod_map = lambda i: (i // tpb, 0, 0)
    else:
        mod_map = lambda i: (mod_row, 0, 0)
    tok = lambda w_: pl.BlockSpec((tm, w_), lambda i: (i, 0))
    return pl.pallas_call(
        _out_kernel,
        grid=(xf.shape[0] // tm,),
        in_specs=[tok(D_MODEL), pl.BlockSpec((1, 3, D_MODEL), mod_map), tok(256), tok(384), tok(384),
                  pl.BlockSpec((D_MODEL, D_MODEL), lambda i: (0, 0))],
        out_specs=tok(D_MODEL),
        out_shape=jax.ShapeDtypeStruct(xf.shape, F32),
        compiler_params=pltpu.CompilerParams(
            dimension_semantics=("arbitrary",), vmem_limit_bytes=VMEM_LIMIT),
        name=name,
    )(xf, mod, ya, yb, yc, w)


def _rope_tables(seq):
    t = np.arange(seq)
    row = (t // GRID_W).astype(np.float32)
    col = (t % GRID_W).astype(np.float32)
    out = []
    for d in (A_DK, HEAD_DIM):
        nf = d // 4
        inv = (np.float32(ROPE_THETA) ** (-np.arange(nf, dtype=np.float32) / np.float32(nf))).astype(np.float32)
        ang = np.concatenate([row[:, None] * inv, col[:, None] * inv], axis=-1).astype(np.float32)
        cos = np.cos(ang).astype(np.float32)
        sin = np.sin(ang).astype(np.float32)
        cos_h = np.concatenate([cos, cos], axis=-1)
        sin_h = np.concatenate([-sin, sin], axis=-1)
        out.append(np.tile(cos_h, (1, LANES // d)))
        out.append(np.tile(sin_h, (1, LANES // d)))
    return np.stack(out).astype(np.float32)


def _block_diag(d):
    i = np.arange(MXU_W)
    return (i[:, None] // d == i[None, :] // d).astype(np.float32)


_B_HEADS_MAP = tuple(tuple(((h // 3) * 2 + (h % 2), h // 3) for h in (2 * g, 2 * g + 1)) for g in range(3))
_C_HEADS_MAP = tuple(tuple((h, h) for h in (2 * g, 2 * g + 1)) for g in range(3))


def kernel(x, c, ctx, c_ctx, norm_g, w_ada, b_ada, w_in, w_out, diff_q_norm, diff_k_norm, lambda_q1, lambda_k1,
           lambda_q2, lambda_k2, diff_subln, gqa_q_norm, gqa_k_norm, nat_q_norm, nat_k_norm, nat_rpb):
    nb, seq, _ = x.shape
    n_ctx = ctx.shape[1]
    rows = seq // GRID_W

    cc = jnp.concatenate([c, c_ctx[None, :], jnp.zeros((16 - nb - 1, D_MODEL), F32)], axis=0)
    mod_all = _ada_call(cc, w_ada, b_ada.reshape(DEPTH, 1, 3 * D_MODEL)).reshape(DEPTH, 16, 3, D_MODEL)

    rope_lat = jnp.asarray(_rope_tables(seq))
    rope_ctx = jnp.zeros((4, n_ctx, LANES), F32)
    bd32 = jnp.asarray(_block_diag(A_DK), BF16)
    bd64 = jnp.asarray(_block_diag(HEAD_DIM), BF16)

    xf = x.reshape(nb * seq, D_MODEL)
    cf = ctx.reshape(nb * n_ctx, D_MODEL)
    ones128 = jnp.ones((LANES,), F32)
    pending = None

    for l in range(DEPTH):
        lam_init = 0.8 - 0.6 * math.exp(-0.3 * l)
        last = l == DEPTH - 1
        w_l = w_in[l].astype(BF16)
        wo_l = w_out[l].astype(BF16)
        mod = mod_all[l]
        ng = norm_g[l].reshape(1, D_MODEL)
        sa = A_DK ** -0.5 * LOG2E
        sb = HEAD_DIM ** -0.5 * LOG2E
        gain = jnp.concatenate([
            jnp.tile(diff_q_norm[l], 8) * sa, jnp.tile(diff_k_norm[l], 8), ones128, ones128, ones128, ones128,
            jnp.tile(gqa_q_norm[l], 6) * sb, jnp.tile(gqa_k_norm[l], 2), ones128, ones128, ones128, ones128,
            jnp.tile(nat_q_norm[l], 6) * sb, jnp.tile(nat_k_norm[l], 6),
            ones128, ones128, ones128, ones128, ones128, ones128]).reshape(1, IN_W)
        lamp = jnp.zeros((8, LANES), F32)
        lamp = lamp.at[0, :A_DK].set(lambda_q1[l]).at[1, :A_DK].set(lambda_k1[l])
        lamp = lamp.at[2, :A_DK].set(lambda_q2[l]).at[3, :A_DK].set(lambda_k2[l])
        sub = jnp.tile(diff_subln[l], 2).reshape(1, LANES)
        bias = _bias_call(nat_rpb[l].reshape(-1), rows)

        lat = _proj_call(xf, mod, ng, w_l, gain, rope_lat, bd32, bd64,
                         nb=nb, seq=seq, tm=512, rope=True, mod_row=None, fuse=pending)
        if pending is not None:
            xf = lat[12]
        cx = _proj_call(cf, mod, ng, w_l, gain, rope_ctx, bd32, bd64,
                        nb=nb, seq=n_ctx, tm=n_ctx, rope=False, mod_row=nb)
        qa, qb, qc, kta, ktb, ktc, va, vb, vc, ga, gb, gc = lat[:12]
        qa_c, qb_c, qc_c, kta_c, ktb_c, ktc_c, va_c, vb_c, vc_c, ga_c, gb_c, gc_c = cx

        ya = _attn_a_call(lamp, sub, qa, ga, kta_c, va_c, kta, va, nb=nb, sq=seq, tq=256, lam_init=lam_init, n_sub=2)
        yb = _attn_plain_call(qb, gb, ktb_c, vb_c, ktb, vb, nb=nb, sq=seq, tq=256, heads=_B_HEADS_MAP,
                              name="attn_b_lat", n_sub=2)
        yc = _attn_c_call(qc, gc, ktc_c, vc_c, ktc, vc, bias, nb=nb, sq=seq)
        if last:
            xf = _out_call(xf, mod, ya, yb, yc, wo_l, seq=seq, tm=512, mod_row=None, name="out_lat")
        else:
            pending = (ya, yb, yc, wo_l, mod)

        if not last:
            cf = _ctx_call(cf, mod, lamp, sub, cx, wo_l, nb=nb, n_ctx=n_ctx, mod_row=nb, lam_init=lam_init)

    return xf.reshape(nb, seq, D_MODEL)
```

```python
import functools
import math

import numpy as np
import jax
import jax.numpy as jnp
from jax import lax
from jax.experimental import pallas as pl
from jax.experimental.pallas import tpu as pltpu

D_MODEL = 1024
DEPTH = 4
GRID_W = 64
HEAD_DIM = 64
A_HEADS = 4
A_DK = 32
B_HEADS = 6
B_KV_HEADS = 2
C_HEADS = 6
WIN_H = 8
WIN_W = 16
ROPE_THETA = 10000.0
EPS = 1e-6
IN_W = 3584

LANES = 128
MXU_W = 256
LOG2E = 1.4426950408889634
NEG = -1e30
VMEM_LIMIT = 56 * 1024 * 1024

F32 = jnp.float32
BF16 = jnp.bfloat16

_ROLES = (["qa"] * 2 + ["ka"] * 2 + ["va"] * 2 + ["ga"] * 2 + ["qb"] * 3 + ["kb"] + ["vb"] + ["gb"] * 3
          + ["qn"] * 3 + ["kn"] * 3 + ["vn"] * 3 + ["gn"] * 3)
_NORM_D = {"qa": A_DK, "ka": A_DK, "qb": HEAD_DIM, "kb": HEAD_DIM, "qn": HEAD_DIM, "kn": HEAD_DIM}
_ROPE_ROLES = ("qa", "ka", "qb", "kb")


def _dot(a, b):
    return jnp.dot(a, b, preferred_element_type=F32)


def _silu(x):
    return x * (1.0 / (1.0 + jnp.exp(-x)))


def _ada_kernel(c_ref, w_ref, b_ref, o_ref):
    a = _silu(c_ref[...])
    o_ref[0] = jnp.dot(a, w_ref[0], preferred_element_type=F32,
                       precision=lax.Precision.HIGHEST) + b_ref[0]


def _ada_call(cc, w_ada, b_ada):
    n = cc.shape[0]
    nblk = 3 * D_MODEL // D_MODEL
    return pl.pallas_call(
        _ada_kernel,
        grid=(DEPTH, nblk),
        in_specs=[
            pl.BlockSpec((n, D_MODEL), lambda l, j: (0, 0)),
            pl.BlockSpec((1, D_MODEL, D_MODEL), lambda l, j: (l, 0, j)),
            pl.BlockSpec((1, 1, D_MODEL), lambda l, j: (l, 0, j)),
        ],
        out_specs=pl.BlockSpec((1, n, D_MODEL), lambda l, j: (l, 0, j)),
        out_shape=jax.ShapeDtypeStruct((DEPTH, n, 3 * D_MODEL), F32),
        compiler_params=pltpu.CompilerParams(
            dimension_semantics=("arbitrary", "arbitrary"), vmem_limit_bytes=VMEM_LIMIT),
        name="ada",
    )(cc, w_ada, b_ada)


def _proj_kernel(*refs, rope, tm, fuse_out):
    x_ref, mod_ref, ng_ref, w_ref, gain_ref, rope_ref, bd32_ref, bd64_ref = refs[:8]
    refs = refs[8:]
    if fuse_out:
        ya_ref, yb_ref, yc_ref, wo_ref, modp_ref = refs[:5]
        refs = refs[5:]
    (qa_ref, qb_ref, qc_ref, kta_ref, ktb_ref, ktc_ref, va_ref, vb_ref, vc_ref,
     ga_ref, gb_ref, gc_ref) = refs[:12]
    x = x_ref[...]
    if fuse_out:
        yprev = jnp.concatenate([ya_ref[...], yb_ref[...], yc_ref[...]], axis=1)
        x = x + modp_ref[0, 2:3, :] * _dot(yprev, wo_ref[...])
        refs[12][...] = x
    ms = jnp.mean(x * x, axis=-1, keepdims=True)
    xn = x * lax.rsqrt(ms + EPS) * ng_ref[...]
    h = xn * (1.0 + mod_ref[0, 1:2, :]) + mod_ref[0, 0:1, :]
    hb = h.astype(BF16)

    lane = lax.broadcasted_iota(jnp.int32, (1, LANES), 1)
    row = lax.broadcasted_iota(jnp.int32, (LANES, 1), 0)
    first_half = {A_DK: (lane % A_DK) < (A_DK // 2), HEAD_DIM: (lane % HEAD_DIM) < (HEAD_DIM // 2)}
    bd = {A_DK: bd32_ref, HEAD_DIM: bd64_ref}
    rope_idx = {A_DK: 0, HEAD_DIM: 2}
    count = {}

    n_ch = IN_W // MXU_W
    y_next = _dot(hb, w_ref[:, 0:MXU_W])
    for ch in range(n_ch):
        c0 = ch * MXU_W
        y = y_next
        if ch + 1 < n_ch:
            y_next = _dot(hb, w_ref[:, c0 + MXU_W:c0 + 2 * MXU_W])
        roles = _ROLES[2 * ch:2 * ch + 2]
        d = _NORM_D.get(roles[0]) or _NORM_D.get(roles[1])
        if d is not None:
            ss = _dot((y * y).astype(BF16), bd[d][...])
            yn = y * lax.rsqrt(ss * (1.0 / d) + EPS) * gain_ref[:, c0:c0 + MXU_W]
        for half in range(2):
            role = roles[half]
            j = count.get(role, 0)
            count[role] = j + 1
            sl = slice(half * LANES, (half + 1) * LANES)
            if role in _NORM_D:
                z = yn[:, sl]
                if rope and role in _ROPE_ROLES:
                    dd = _NORM_D[role]
                    zsw = jnp.where(first_half[dd], pltpu.roll(z, LANES - dd // 2, 1),
                                    pltpu.roll(z, dd // 2, 1))
                    z = z * rope_ref[rope_idx[dd]] + zsw * rope_ref[rope_idx[dd] + 1]
            else:
                z = y[:, sl]
            osl = slice(j * LANES, (j + 1) * LANES)
            if role == "qa":
                qa_ref[:, osl] = z.astype(BF16)
            elif role == "qb":
                qb_ref[:, osl] = z.astype(BF16)
            elif role == "qn":
                qc_ref[:, osl] = z.astype(BF16)
            elif role == "ka":
                zt = z.T
                for s in range(LANES // A_DK):
                    kta_ref[0, 4 * j + s] = jnp.where(row // A_DK == s, zt, 0.0).astype(BF16)
            elif role == "kb":
                zt = z.T
                zst = pltpu.roll(z, HEAD_DIM, 1).T
                lo = row < HEAD_DIM
                ktb_ref[0, 0] = jnp.where(lo, zt, 0.0).astype(BF16)
                ktb_ref[0, 1] = jnp.where(lo, 0.0, zst).astype(BF16)
                ktb_ref[0, 2] = jnp.where(lo, zst, 0.0).astype(BF16)
                ktb_ref[0, 3] = jnp.where(lo, 0.0, zt).astype(BF16)
            elif role == "kn":
                zt = z.T
                lo = row < HEAD_DIM
                ktc_ref[0, 2 * j] = jnp.where(lo, zt, 0.0).astype(BF16)
                ktc_ref[0, 2 * j + 1] = jnp.where(lo, 0.0, zt).astype(BF16)
            elif role in ("va", "vb", "vn"):
                v_ref = {"va": va_ref, "vb": vb_ref, "vn": vc_ref}[role]
                lo = lane < HEAD_DIM
                v_ref[0, 2 * j] = jnp.where(lo, z, 1.0).astype(BF16)
                v_ref[0, 2 * j + 1] = jnp.where(lo, pltpu.roll(z, HEAD_DIM, 1), 1.0).astype(BF16)
            elif role == "ga":
                ga_ref[:, osl] = _silu(z)
            elif role == "gb":
                gb_ref[:, osl] = _silu(z)
            elif role == "gn":
                gc_ref[:, osl] = _silu(z)


def _proj_call(xf, mod, ng, w, gain, rope_tab, bd32, bd64, *, nb, seq, tm, rope, mod_row, fuse=None):
    tpb = seq // tm
    t_all = nb * seq
    if mod_row is None:
        mod_map = lambda i: (i // tpb, 0, 0)
    else:
        mod_map = lambda i: (mod_row, 0, 0)
    tok = lambda w_: pl.BlockSpec((tm, w_), lambda i: (i, 0))
    const2 = lambda shp: pl.BlockSpec(shp, lambda i: (0, 0))
    in_specs = [
        tok(D_MODEL),
        pl.BlockSpec((1, 3, D_MODEL), mod_map),
        const2((1, D_MODEL)),
        const2((D_MODEL, IN_W)),
        const2((1, IN_W)),
        pl.BlockSpec((4, tm, LANES), lambda i: (0, i % tpb, 0)),
        const2((MXU_W, MXU_W)),
        const2((MXU_W, MXU_W)),
    ]
    out_shape = [
        jax.ShapeDtypeStruct((t_all, 256), BF16),
        jax.ShapeDtypeStruct((t_all, 384), BF16),
        jax.ShapeDtypeStruct((t_all, 384), BF16),
        jax.ShapeDtypeStruct((nb, 8, LANES, seq), BF16),
        jax.ShapeDtypeStruct((nb, 4, LANES, seq), BF16),
        jax.ShapeDtypeStruct((nb, 6, LANES, seq), BF16),
        jax.ShapeDtypeStruct((nb, 4, seq, LANES), BF16),
        jax.ShapeDtypeStruct((nb, 2, seq, LANES), BF16),
        jax.ShapeDtypeStruct((nb, 6, seq, LANES), BF16),
        jax.ShapeDtypeStruct((t_all, 256), F32),
        jax.ShapeDtypeStruct((t_all, 384), F32),
        jax.ShapeDtypeStruct((t_all, 384), F32),
    ]
    out_specs = [
        tok(256), tok(384), tok(384),
        pl.BlockSpec((1, 8, LANES, tm), lambda i: (i // tpb, 0, 0, i % tpb)),
        pl.BlockSpec((1, 4, LANES, tm), lambda i: (i // tpb, 0, 0, i % tpb)),
        pl.BlockSpec((1, 6, LANES, tm), lambda i: (i // tpb, 0, 0, i % tpb)),
        pl.BlockSpec((1, 4, tm, LANES), lambda i: (i // tpb, 0, i % tpb, 0)),
        pl.BlockSpec((1, 2, tm, LANES), lambda i: (i // tpb, 0, i % tpb, 0)),
        pl.BlockSpec((1, 6, tm, LANES), lambda i: (i // tpb, 0, i % tpb, 0)),
        tok(256), tok(384), tok(384),
    ]
    args = [xf, mod, ng, w, gain, rope_tab, bd32, bd64]
    if fuse is not None:
        ya, yb, yc, wo, modp = fuse
        in_specs += [tok(ya.shape[1]), tok(yb.shape[1]), tok(yc.shape[1]), const2(wo.shape),
                     pl.BlockSpec((1, 3, D_MODEL), mod_map)]
        out_shape.append(jax.ShapeDtypeStruct(xf.shape, F32))
        out_specs.append(tok(D_MODEL))
        args += [ya, yb, yc, wo, modp]
    return pl.pallas_call(
        functools.partial(_proj_kernel, rope=rope, tm=tm, fuse_out=fuse is not None),
        grid=(t_all // tm,),
        in_specs=in_specs,
        out_specs=out_specs,
        out_shape=out_shape,
        compiler_params=pltpu.CompilerParams(
            dimension_semantics=("arbitrary",), vmem_limit_bytes=VMEM_LIMIT),
        name="proj_lat" if rope else "proj_ctx",
    )(*args)


CHUNK = MXU_W
ATTN_CHUNK = 512
ATTN_TQ = 256
ATTN_N_SUB = 4


def _fold_lanes(x, op):
    acc = x[:, :LANES]
    for i in range(1, x.shape[1] // LANES):
        acc = op(acc, x[:, i * LANES:(i + 1) * LANES])
    return acc


def _run_units(n_units, chunks, score_chunk, value_chunk, s_scr, finish):
    def qk(u, c, m_run):
        k0, w = chunks[c]
        s = score_chunk(u, c)
        s_scr[u % 2, :, k0:k0 + w] = s
        part = _fold_lanes(s, jnp.maximum)
        return part if m_run is None else jnp.maximum(m_run, part)

    def sm(u, c, m, o):
        k0, w = chunks[c]
        p = jnp.exp2(s_scr[u % 2, :, k0:k0 + w] - m)
        d = _dot(p.astype(BF16), value_chunk(u, c))
        return d if o is None else o + d

    m_run = None
    for c in range(len(chunks)):
        m_run = qk(0, c, m_run)
    for u in range(n_units):
        m = jnp.max(m_run, axis=-1, keepdims=True)
        m_run, o = None, None
        for c in range(len(chunks)):
            if u + 1 < n_units:
                m_run = qk(u + 1, c, m_run)
            o = sm(u, c, m, o)
        finish(u, o)


def _normalise(o, half):
    r = pltpu.roll(o, HEAD_DIM, 1)
    return o * (1.0 / r) if half == 0 else r * (1.0 / o)


def _key_chunks(n_lat_keys, n_ctx_keys, width):
    chunks = [(k, width) for k in range(0, n_lat_keys, width)]
    return chunks + [(n_lat_keys, n_ctx_keys)]


def _diff_finisher(lam, lam_init, sub_ref, lane, emit):
    state = {}

    def finish(a, o):
        h, m = divmod(a, 2)
        grp, half = divmod(h, 2)
        o = _normalise(o, half)
        if m == 0:
            state["o0"] = o
            return
        o = jnp.where(lane // HEAD_DIM == half, state["o0"] - lam * o, 0.0)
        msq = jnp.sum(o * o, axis=-1, keepdims=True) * (1.0 / HEAD_DIM)
        yh = o * lax.rsqrt(msq + EPS)
        if half == 0:
            state["yg"] = yh
            return
        emit(grp, (state["yg"] + yh) * sub_ref[...] * (1.0 - lam_init))

    return finish


def _plain_finisher(lane, emit):
    state = {}

    def finish(grp, half, o):
        o = _normalise(o, half)
        if half == 0:
            state["yg"] = o
            return
        emit(grp, jnp.where(lane < HEAD_DIM, state["yg"], o))

    return finish


def _diff_lambda(lamp_ref, lam_init):
    lp = lamp_ref[...]
    l1 = jnp.sum(lp[0:1] * lp[1:2], axis=-1, keepdims=True)
    l2 = jnp.sum(lp[2:3] * lp[3:4], axis=-1, keepdims=True)
    return jnp.exp(l1) - jnp.exp(l2) + lam_init


def _attn_a_kernel(*refs, has_lat, lam_init, chunk_w, tq):
    if has_lat:
        lamp_ref, sub_ref, q_ref, g_ref, ktc_ref, vc_ref, kt_ref, v_ref, o_ref, s_scr = refs
    else:
        lamp_ref, sub_ref, q_ref, g_ref, ktc_ref, vc_ref, o_ref, s_scr = refs
    n_lat = kt_ref.shape[3] if has_lat else 0
    chunks = _key_chunks(n_lat, ktc_ref.shape[3], chunk_w)
    lam = _diff_lambda(lamp_ref, lam_init)
    lane = lax.broadcasted_iota(jnp.int32, (1, LANES), 1)
    n_sub = q_ref.shape[0] // tq

    def score_chunk(u, c):
        r, a = divmod(u, 2 * A_HEADS)
        grp = a // 4
        k0, w = chunks[c]
        ql = q_ref[r * tq:(r + 1) * tq, grp * LANES:(grp + 1) * LANES]
        if k0 < n_lat:
            return _dot(ql, kt_ref[0, a, :, k0:k0 + w])
        return _dot(ql, ktc_ref[0, a])

    def value_chunk(u, c):
        a = u % (2 * A_HEADS)
        k0, w = chunks[c]
        if k0 < n_lat:
            return v_ref[0, a // 2, k0:k0 + w, :]
        return vc_ref[0, a // 2]

    def make_fin(r):
        rows = slice(r * tq, (r + 1) * tq)

        def emit(grp, y):
            sl = slice(grp * LANES, (grp + 1) * LANES)
            o_ref[rows, sl] = (y * g_ref[rows, sl]).astype(BF16)

        return _diff_finisher(lam, lam_init, sub_ref, lane, emit)

    fin = [make_fin(r) for r in range(n_sub)]

    def finish(u, o):
        r, a = divmod(u, 2 * A_HEADS)
        fin[r](a, o)

    _run_units(n_sub * 2 * A_HEADS, chunks, score_chunk, value_chunk, s_scr, finish)


def _attn_plain_kernel(*refs, has_lat, heads, chunk_w, tq):
    if has_lat:
        q_ref, g_ref, ktc_ref, vc_ref, kt_ref, v_ref, o_ref, s_scr = refs
    else:
        q_ref, g_ref, ktc_ref, vc_ref, o_ref, s_scr = refs
    n_lat = kt_ref.shape[3] if has_lat else 0
    chunks = _key_chunks(n_lat, ktc_ref.shape[-1], chunk_w)
    lane = lax.broadcasted_iota(jnp.int32, (1, LANES), 1)
    n_sub = q_ref.shape[0] // tq
    units = [(r, grp, half, ki, vi) for r in range(n_sub)
             for grp, pair in enumerate(heads) for half, (ki, vi) in enumerate(pair)]

    def score_chunk(u, c):
        r, grp, _, ki, _ = units[u]
        k0, w = chunks[c]
        ql = q_ref[r * tq:(r + 1) * tq, grp * LANES:(grp + 1) * LANES]
        if k0 < n_lat:
            return _dot(ql, kt_ref[0, ki, :, k0:k0 + w])
        return _dot(ql, ktc_ref[0, ki])

    def value_chunk(u, c):
        vi = units[u][4]
        k0, w = chunks[c]
        if k0 < n_lat:
            return v_ref[0, vi, k0:k0 + w, :]
        return vc_ref[0, vi]

    def make_fin(r):
        rows = slice(r * tq, (r + 1) * tq)

        def emit(grp, y):
            sl = slice(grp * LANES, (grp + 1) * LANES)
            o_ref[rows, sl] = (y * g_ref[rows, sl]).astype(BF16)

        return _plain_finisher(lane, emit)

    fin = [make_fin(r) for r in range(n_sub)]

    def finish(u, o):
        r, grp, half = units[u][:3]
        fin[r](grp, half, o)

    _run_units(len(units), chunks, score_chunk, value_chunk, s_scr, finish)


def _attn_c_kernel(q_ref, g_ref, ktc_ref, vc_ref, kt_ref, v_ref, bias_ref, o_ref, s_scr, *, nwin, n_sub):
    n_tiles = pl.num_programs(1) * n_sub
    lane = lax.broadcasted_iota(jnp.int32, (1, LANES), 1)
    win = nwin * CHUNK
    chunks = [(0, win), (win, ktc_ref.shape[3])]
    tile = [pl.program_id(1) * n_sub + r for r in range(n_sub)]
    j0 = [jnp.clip(t - 1, 0, kt_ref.shape[3] // CHUNK - nwin) for t in tile]
    k0 = [pl.multiple_of(j * CHUNK, CHUNK) for j in j0]
    kind = [jnp.where(t == 0, 0, jnp.where(t == n_tiles - 1, 2, 1)) for t in tile]

    def score_chunk(u, c):
        r, h = divmod(u, C_HEADS)
        grp = h // 2
        ql = q_ref[r * _C_TQ:(r + 1) * _C_TQ, grp * LANES:(grp + 1) * LANES]
        if c == 0:
            return _dot(ql, kt_ref[0, h, :, pl.ds(k0[r], win)]) + bias_ref[kind[r], h]
        return _dot(ql, ktc_ref[0, h])

    def value_chunk(u, c):
        r, h = divmod(u, C_HEADS)
        if c == 0:
            return v_ref[0, h, pl.ds(k0[r], win), :]
        return vc_ref[0, h]

    def make_fin(r):
        rows = slice(r * _C_TQ, (r + 1) * _C_TQ)

        def emit(grp, y):
            sl = slice(grp * LANES, (grp + 1) * LANES)
            o_ref[rows, sl] = (y * g_ref[rows, sl]).astype(BF16)

        return _plain_finisher(lane, emit)

    fin = [make_fin(r) for r in range(n_sub)]

    def finish(u, o):
        r, h = divmod(u, C_HEADS)
        fin[r](h // 2, h % 2, o)

    _run_units(n_sub * C_HEADS, chunks, score_chunk, value_chunk, s_scr, finish)


def _attn_specs(width, tq, nq_tiles, kt_shape, v_shape, ktc_shape, vc_shape, has_lat):
    tokq = pl.BlockSpec((tq, width), lambda b, t: (b * nq_tiles + t, 0))

    def per_batch(shape):
        nd = len(shape)
        return pl.BlockSpec((1,) + tuple(shape[1:]), lambda b, t: (b,) + (0,) * (nd - 1))

    specs = [tokq, tokq, per_batch(ktc_shape), per_batch(vc_shape)]
    if has_lat:
        specs += [per_batch(kt_shape), per_batch(v_shape)]
    return specs, tokq


def _score_scratch(tq, n_keys):
    return [pltpu.VMEM((2, tq, n_keys), F32)]


def _attn_a_call(lamp, sub, q, g, ktc, vc, kt, v, *, nb, sq, tq, lam_init, n_sub=1):
    has_lat = kt is not None
    nqt = sq // (tq * n_sub)
    n_keys = ktc.shape[-1] + (kt.shape[-1] if has_lat else 0)
    specs, tokq = _attn_specs(256, tq * n_sub, nqt, kt.shape if has_lat else None, v.shape if has_lat else None,
                              ktc.shape, vc.shape, has_lat)
    small = [pl.BlockSpec((8, LANES), lambda b, t: (0, 0)), pl.BlockSpec((1, LANES), lambda b, t: (0, 0))]
    args = [lamp, sub, q, g, ktc, vc] + ([kt, v] if has_lat else [])
    return pl.pallas_call(
        functools.partial(_attn_a_kernel, has_lat=has_lat, lam_init=lam_init, chunk_w=ATTN_CHUNK, tq=tq),
        grid=(nb, nqt),
        in_specs=small + specs,
        out_specs=tokq,
        out_shape=jax.ShapeDtypeStruct((nb * sq, 256), BF16),
        scratch_shapes=_score_scratch(tq, n_keys),
        compiler_params=pltpu.CompilerParams(
            dimension_semantics=("arbitrary", "arbitrary"), vmem_limit_bytes=VMEM_LIMIT),
        name="attn_a_lat" if has_lat else "attn_a_ctx",
    )(*args)


def _attn_plain_call(q, g, ktc, vc, kt, v, *, nb, sq, tq, heads, name, n_sub=1):
    has_lat = kt is not None
    nqt = sq // (tq * n_sub)
    n_keys = ktc.shape[-1] + (kt.shape[-1] if has_lat else 0)
    specs, tokq = _attn_specs(384, tq * n_sub, nqt, kt.shape if has_lat else None, v.shape if has_lat else None,
                              ktc.shape, vc.shape, has_lat)
    args = [q, g, ktc, vc] + ([kt, v] if has_lat else [])
    return pl.pallas_call(
        functools.partial(_attn_plain_kernel, has_lat=has_lat, heads=heads, chunk_w=ATTN_CHUNK, tq=tq),
        grid=(nb, nqt),
        in_specs=specs,
        out_specs=tokq,
        out_shape=jax.ShapeDtypeStruct((nb * sq, 384), BF16),
        scratch_shapes=_score_scratch(tq, n_keys),
        compiler_params=pltpu.CompilerParams(
            dimension_semantics=("arbitrary", "arbitrary"), vmem_limit_bytes=VMEM_LIMIT),
        name=name,
    )(*args)


_C_TQ = 4 * GRID_W
_C_NWIN = 3


def _attn_c_call(q, g, ktc, vc, kt, v, bias, *, nb, sq, n_sub):
    nqt = sq // (_C_TQ * n_sub)
    specs, tokq = _attn_specs(384, _C_TQ * n_sub, nqt, kt.shape, v.shape, ktc.shape, vc.shape, True)
    specs.append(pl.BlockSpec(bias.shape, lambda b, t: (0, 0, 0, 0), pipeline_mode=pl.Buffered(1)))
    return pl.pallas_call(
        functools.partial(_attn_c_kernel, nwin=_C_NWIN, n_sub=n_sub),
        grid=(nb, nqt),
        in_specs=specs,
        out_specs=tokq,
        out_shape=jax.ShapeDtypeStruct((nb * sq, 384), BF16),
        scratch_shapes=_score_scratch(_C_TQ, (_C_NWIN + 1) * CHUNK),
        compiler_params=pltpu.CompilerParams(
            dimension_semantics=("arbitrary", "arbitrary"), vmem_limit_bytes=VMEM_LIMIT),
        name="attn_c_lat",
    )(q, g, ktc, vc, kt, v, bias)


def _ctx_kernel(x_ref, mod_ref, lamp_ref, sub_ref, qa_ref, qb_ref, qc_ref, ga_ref, gb_ref, gc_ref,
                kta_ref, ktb_ref, ktc_ref, va_ref, vb_ref, vc_ref, w_ref, o_ref, s_scr, y_scr, *, lam_init):
    lam = _diff_lambda(lamp_ref, lam_init)
    lane = lax.broadcasted_iota(jnp.int32, (1, LANES), 1)
    chunks = [(0, kta_ref.shape[3])]
    b_units = [(grp, half, ki, vi) for grp, pair in enumerate(_B_HEADS_MAP) for half, (ki, vi) in enumerate(pair)]
    n_a, n_b = 2 * A_HEADS, len(b_units)

    def score_chunk(u, c):
        if u < n_a:
            return _dot(qa_ref[:, (u // 4) * LANES:(u // 4 + 1) * LANES], kta_ref[0, u])
        if u < n_a + n_b:
            grp, _, ki, _ = b_units[u - n_a]
            return _dot(qb_ref[:, grp * LANES:(grp + 1) * LANES], ktb_ref[0, ki])
        h = u - n_a - n_b
        return _dot(qc_ref[:, (h // 2) * LANES:(h // 2 + 1) * LANES], ktc_ref[0, h])

    def value_chunk(u, c):
        if u < n_a:
            return va_ref[0, u // 2]
        if u < n_a + n_b:
            return vb_ref[0, b_units[u - n_a][3]]
        return vc_ref[0, u - n_a - n_b]

    def emitter(g_ref, col0):
        def emit(grp, y):
            sl = slice(grp * LANES, (grp + 1) * LANES)
            y_scr[:, col0 + grp * LANES:col0 + (grp + 1) * LANES] = (y * g_ref[:, sl]).astype(BF16)
        return emit

    fin_a = _diff_finisher(lam, lam_init, sub_ref, lane, emitter(ga_ref, 0))
    fin_b = _plain_finisher(lane, emitter(gb_ref, qa_ref.shape[1]))
    fin_c = _plain_finisher(lane, emitter(gc_ref, qa_ref.shape[1] + qb_ref.shape[1]))

    def finish(u, o):
        if u < n_a:
            fin_a(u, o)
        elif u < n_a + n_b:
            grp, half = b_units[u - n_a][:2]
            fin_b(grp, half, o)
        else:
            h = u - n_a - n_b
            fin_c(h // 2, h % 2, o)

    _run_units(n_a + n_b + C_HEADS, chunks, score_chunk, value_chunk, s_scr, finish)
    o_ref[...] = x_ref[...] + mod_ref[0, 2:3, :] * _dot(y_scr[...], w_ref[...])


def _ctx_call(cf, mod, lamp, sub, cx, w, *, nb, n_ctx, mod_row, lam_init):
    qa, qb, qc, kta, ktb, ktc, va, vb, vc, ga, gb, gc = cx
    tok = lambda w_: pl.BlockSpec((n_ctx, w_), lambda b: (b, 0))
    per_b = lambda a: pl.BlockSpec((1,) + tuple(a.shape[1:]), lambda b: (b, 0, 0, 0))
    const = lambda shp: pl.BlockSpec(shp, lambda b: (0,) * len(shp))
    in_specs = [tok(D_MODEL), pl.BlockSpec((1, 3, D_MODEL), lambda b: (mod_row, 0, 0)),
                const((8, LANES)), const((1, LANES)),
                tok(qa.shape[1]), tok(qb.shape[1]), tok(qc.shape[1]),
                tok(ga.shape[1]), tok(gb.shape[1]), tok(gc.shape[1]),
                per_b(kta), per_b(ktb), per_b(ktc), per_b(va), per_b(vb), per_b(vc),
                const((D_MODEL, D_MODEL))]
    return pl.pallas_call(
        functools.partial(_ctx_kernel, lam_init=lam_init),
        grid=(nb,),
        in_specs=in_specs,
        out_specs=tok(D_MODEL),
        out_shape=jax.ShapeDtypeStruct(cf.shape, F32),
        scratch_shapes=[pltpu.VMEM((2, n_ctx, n_ctx), F32), pltpu.VMEM((n_ctx, D_MODEL), BF16)],
        compiler_params=pltpu.CompilerParams(
            dimension_semantics=("arbitrary",), vmem_limit_bytes=VMEM_LIMIT),
        name="ctx_block",
    )(cf, mod, lamp, sub, qa, qb, qc, ga, gb, gc, kta, ktb, ktc, va, vb, vc, w)


_RPB_H = 2 * WIN_H - 1
_RPB_W = 2 * WIN_W - 1


def _bias_kernel(rpb_ref, o_ref, *, rows):
    base = pl.program_id(0) * (_RPB_H * _RPB_W)
    lane = lax.broadcasted_iota(jnp.int32, (GRID_W, LANES), 1)
    qc = lax.broadcasted_iota(jnp.int32, (GRID_W, LANES), 0)
    kc = lane % GRID_W
    dcol = kc - qc + (WIN_W - 1)
    c0 = jnp.clip(qc - WIN_W // 2, 0, GRID_W - WIN_W)
    ok_c = (kc >= c0) & (kc < c0 + WIN_W)
    lo = lane < GRID_W
    kh = min(WIN_H, rows)
    nqt = rows // 4
    tables = []
    for dr in range(_RPB_H):
        acc = jnp.full((GRID_W, LANES), rpb_ref[base + dr * _RPB_W], F32)
        for dc in range(1, _RPB_W):
            acc = jnp.where(dcol >= dc, rpb_ref[base + dr * _RPB_W + dc], acc)
        tables.append(acc * LOG2E)
    for kind, t in enumerate((0, 1, nqt - 1)):
        ks = min(max(4 * t - 4, 0), rows - 4 * _C_NWIN)
        for i in range(4):
            qr = 4 * t + i
            r0 = min(max(qr - kh // 2, 0), rows - kh)
            for jj in range(_C_NWIN * 2):
                kra = ks + 2 * jj
                ok_a = r0 <= kra < r0 + kh
                ok_b = r0 <= kra + 1 < r0 + kh
                dra = kra - qr + (WIN_H - 1)
                if ok_a and ok_b:
                    piece = jnp.where(ok_c, jnp.where(lo, tables[dra], tables[dra + 1]), NEG)
                elif ok_a:
                    piece = jnp.where(ok_c & lo, tables[dra], NEG)
                elif ok_b:
                    piece = jnp.where(ok_c & jnp.logical_not(lo), tables[dra + 1], NEG)
                else:
                    piece = jnp.full((GRID_W, LANES), NEG, F32)
                o_ref[kind, 0, i * GRID_W:(i + 1) * GRID_W, jj * LANES:(jj + 1) * LANES] = piece


def _bias_call(rpb_flat, rows):
    return pl.pallas_call(
        functools.partial(_bias_kernel, rows=rows),
        grid=(C_HEADS,),
        in_specs=[pl.BlockSpec(memory_space=pltpu.SMEM)],
        out_specs=pl.BlockSpec((3, 1, _C_TQ, _C_NWIN * MXU_W), lambda h: (0, h, 0, 0)),
        out_shape=jax.ShapeDtypeStruct((3, C_HEADS, _C_TQ, _C_NWIN * MXU_W), F32),
        compiler_params=pltpu.CompilerParams(
            dimension_semantics=("arbitrary",), vmem_limit_bytes=VMEM_LIMIT),
        name="nat_bias",
    )(rpb_flat)


def _out_kernel(x_ref, mod_ref, ya_ref, yb_ref, yc_ref, w_ref, o_ref):
    y = jnp.concatenate([ya_ref[...], yb_ref[...], yc_ref[...]], axis=1)
    o_ref[...] = x_ref[...] + mod_ref[0, 2:3, :] * _dot(y, w_ref[...])


def _out_call(xf, mod, ya, yb, yc, w, *, seq, tm, mod_row, name):
    tpb = seq // tm
    if mod_row is None:
        mod_map = lambda i: (i // tpb, 0, 0)
    else:
        mod_map = lambda i: (mod_row, 0, 0)
    tok = lambda w_: pl.BlockSpec((tm, w_), lambda i: (i, 0))
    return pl.pallas_call(
        _out_kernel,
        grid=(xf.shape[0] // tm,),
        in_specs=[tok(D_MODEL), pl.BlockSpec((1, 3, D_MODEL), mod_map), tok(256), tok(384), tok(384),
                  pl.BlockSpec((D_MODEL, D_MODEL), lambda i: (0, 0))],
        out_specs=tok(D_MODEL),
        out_shape=jax.ShapeDtypeStruct(xf.shape, F32),
        compiler_params=pltpu.CompilerParams(
            dimension_semantics=("arbitrary",), vmem_limit_bytes=VMEM_LIMIT),
        name=name,
    )(xf, mod, ya, yb, yc, w)


def _rope_tables(seq):
    t = np.arange(seq)
    row = (t // GRID_W).astype(np.float32)
    col = (t % GRID_W).astype(np.float32)
    out = []
    for d in (A_DK, HEAD_DIM):
        nf = d // 4
        inv = (np.float32(ROPE_THETA) ** (-np.arange(nf, dtype=np.float32) / np.float32(nf))).astype(np.float32)
        ang = np.concatenate([row[:, None] * inv, col[:, None] * inv], axis=-1).astype(np.float32)
        cos = np.cos(ang).astype(np.float32)
        sin = np.sin(ang).astype(np.float32)
        cos_h = np.concatenate([cos, cos], axis=-1)
        sin_h = np.concatenate([-sin, sin], axis=-1)
        out.append(np.tile(cos_h, (1, LANES // d)))
        out.append(np.tile(sin_h, (1, LANES // d)))
    return np.stack(out).astype(np.float32)


def _block_diag(d):
    i = np.arange(MXU_W)
    return (i[:, None] // d == i[None, :] // d).astype(np.float32)


_B_HEADS_MAP = tuple(tuple(((h // 3) * 2 + (h % 2), h // 3) for h in (2 * g, 2 * g + 1)) for g in range(3))
_C_HEADS_MAP = tuple(tuple((h, h) for h in (2 * g, 2 * g + 1)) for g in range(3))


def kernel(x, c, ctx, c_ctx, norm_g, w_ada, b_ada, w_in, w_out, diff_q_norm, diff_k_norm, lambda_q1, lambda_k1,
           lambda_q2, lambda_k2, diff_subln, gqa_q_norm, gqa_k_norm, nat_q_norm, nat_k_norm, nat_rpb):
    nb, seq, _ = x.shape
    n_ctx = ctx.shape[1]
    rows = seq // GRID_W

    cc = jnp.concatenate([c, c_ctx[None, :], jnp.zeros((16 - nb - 1, D_MODEL), F32)], axis=0)
    mod_all = _ada_call(cc, w_ada, b_ada.reshape(DEPTH, 1, 3 * D_MODEL)).reshape(DEPTH, 16, 3, D_MODEL)

    rope_lat = jnp.asarray(_rope_tables(seq))
    rope_ctx = jnp.zeros((4, n_ctx, LANES), F32)
    bd32 = jnp.asarray(_block_diag(A_DK), BF16)
    bd64 = jnp.asarray(_block_diag(HEAD_DIM), BF16)

    xf = x.reshape(nb * seq, D_MODEL)
    cf = ctx.reshape(nb * n_ctx, D_MODEL)
    ones128 = jnp.ones((LANES,), F32)
    pending = None

    for l in range(DEPTH):
        lam_init = 0.8 - 0.6 * math.exp(-0.3 * l)
        last = l == DEPTH - 1
        w_l = w_in[l].astype(BF16)
        wo_l = w_out[l].astype(BF16)
        mod = mod_all[l]
        ng = norm_g[l].reshape(1, D_MODEL)
        sa = A_DK ** -0.5 * LOG2E
        sb = HEAD_DIM ** -0.5 * LOG2E
        gain = jnp.concatenate([
            jnp.tile(diff_q_norm[l], 8) * sa, jnp.tile(diff_k_norm[l], 8), ones128, ones128, ones128, ones128,
            jnp.tile(gqa_q_norm[l], 6) * sb, jnp.tile(gqa_k_norm[l], 2), ones128, ones128, ones128, ones128,
            jnp.tile(nat_q_norm[l], 6) * sb, jnp.tile(nat_k_norm[l], 6),
            ones128, ones128, ones128, ones128, ones128, ones128]).reshape(1, IN_W)
        lamp = jnp.zeros((8, LANES), F32)
        lamp = lamp.at[0, :A_DK].set(lambda_q1[l]).at[1, :A_DK].set(lambda_k1[l])
        lamp = lamp.at[2, :A_DK].set(lambda_q2[l]).at[3, :A_DK].set(lambda_k2[l])
        sub = jnp.tile(diff_subln[l], 2).reshape(1, LANES)
        bias = _bias_call(nat_rpb[l].reshape(-1), rows)

        lat = _proj_call(xf, mod, ng, w_l, gain, rope_lat, bd32, bd64,
                         nb=nb, seq=seq, tm=512, rope=True, mod_row=None, fuse=pending)
        if pending is not None:
            xf = lat[12]
        cx = _proj_call(cf, mod, ng, w_l, gain, rope_ctx, bd32, bd64,
                        nb=nb, seq=n_ctx, tm=n_ctx, rope=False, mod_row=nb)
        qa, qb, qc, kta, ktb, ktc, va, vb, vc, ga, gb, gc = lat[:12]
        qa_c, qb_c, qc_c, kta_c, ktb_c, ktc_c, va_c, vb_c, vc_c, ga_c, gb_c, gc_c = cx

        ya = _attn_a_call(lamp, sub, qa, ga, kta_c, va_c, kta, va, nb=nb, sq=seq, tq=ATTN_TQ, lam_init=lam_init,
                          n_sub=ATTN_N_SUB)
        yb = _attn_plain_call(qb, gb, ktb_c, vb_c, ktb, vb, nb=nb, sq=seq, tq=ATTN_TQ, heads=_B_HEADS_MAP,
                              name="attn_b_lat", n_sub=ATTN_N_SUB)
        yc = _attn_c_call(qc, gc, ktc_c, vc_c, ktc, vc, bias, nb=nb, sq=seq, n_sub=ATTN_N_SUB)
        if last:
            xf = _out_call(xf, mod, ya, yb, yc, wo_l, seq=seq, tm=512, mod_row=None, name="out_lat")
        else:
            pending = (ya, yb, yc, wo_l, mod)

        if not last:
            cf = _ctx_call(cf, mod, lamp, sub, cx, wo_l, nb=nb, n_ctx=n_ctx, mod_row=nb, lam_init=lam_init)

    return xf.reshape(nb, seq, D_MODEL)
```

```python
import functools
import math

import numpy as np
import jax
import jax.numpy as jnp
from jax import lax
from jax.experimental import pallas as pl
from jax.experimental.pallas import tpu as pltpu

D_MODEL = 1024
DEPTH = 4
GRID_W = 64
HEAD_DIM = 64
A_HEADS = 4
A_DK = 32
B_HEADS = 6
B_KV_HEADS = 2
C_HEADS = 6
WIN_H = 8
WIN_W = 16
ROPE_THETA = 10000.0
EPS = 1e-6
IN_W = 3584

LANES = 128
MXU_W = 256
LOG2E = 1.4426950408889634
NEG = -1e30
VMEM_LIMIT = 56 * 1024 * 1024

F32 = jnp.float32
BF16 = jnp.bfloat16

_ROLES = (["qa"] * 2 + ["ka"] * 2 + ["va"] * 2 + ["ga"] * 2 + ["qb"] * 3 + ["kb"] + ["vb"] + ["gb"] * 3
          + ["qn"] * 3 + ["kn"] * 3 + ["vn"] * 3 + ["gn"] * 3)
_NORM_D = {"qa": A_DK, "ka": A_DK, "qb": HEAD_DIM, "kb": HEAD_DIM, "qn": HEAD_DIM, "kn": HEAD_DIM}
_ROPE_ROLES = ("qa", "ka", "qb", "kb")


def _dot(a, b):
    return jnp.dot(a, b, preferred_element_type=F32)


def _silu(x):
    return x * (1.0 / (1.0 + jnp.exp(-x)))


def _ada_kernel(c_ref, w_ref, b_ref, o_ref):
    a = _silu(c_ref[...])
    o_ref[0] = jnp.dot(a, w_ref[0], preferred_element_type=F32,
                       precision=lax.Precision.HIGHEST) + b_ref[0]


def _ada_call(cc, w_ada, b_ada):
    n = cc.shape[0]
    nblk = 3 * D_MODEL // D_MODEL
    return pl.pallas_call(
        _ada_kernel,
        grid=(DEPTH, nblk),
        in_specs=[
            pl.BlockSpec((n, D_MODEL), lambda l, j: (0, 0)),
            pl.BlockSpec((1, D_MODEL, D_MODEL), lambda l, j: (l, 0, j)),
            pl.BlockSpec((1, 1, D_MODEL), lambda l, j: (l, 0, j)),
        ],
        out_specs=pl.BlockSpec((1, n, D_MODEL), lambda l, j: (l, 0, j)),
        out_shape=jax.ShapeDtypeStruct((DEPTH, n, 3 * D_MODEL), F32),
        compiler_params=pltpu.CompilerParams(
            dimension_semantics=("arbitrary", "arbitrary"), vmem_limit_bytes=VMEM_LIMIT),
        name="ada",
    )(cc, w_ada, b_ada)


def _proj_kernel(*refs, rope, tm, fuse_out):
    x_ref, mod_ref, ng_ref, w_ref, gain_ref, rope_ref, bd32_ref, bd64_ref = refs[:8]
    refs = refs[8:]
    if fuse_out:
        ya_ref, yb_ref, yc_ref, wo_ref, modp_ref = refs[:5]
        refs = refs[5:]
    (qa_ref, qb_ref, qc_ref, kta_ref, ktb_ref, ktc_ref, va_ref, vb_ref, vc_ref,
     ga_ref, gb_ref, gc_ref) = refs[:12]
    x = x_ref[...]
    if fuse_out:
        yprev = jnp.concatenate([ya_ref[...], yb_ref[...], yc_ref[...]], axis=1)
        x = x + modp_ref[0, 2:3, :] * _dot(yprev, wo_ref[...])
        refs[12][...] = x
    ms = jnp.mean(x * x, axis=-1, keepdims=True)
    xn = x * lax.rsqrt(ms + EPS) * ng_ref[...]
    h = xn * (1.0 + mod_ref[0, 1:2, :]) + mod_ref[0, 0:1, :]
    hb = h.astype(BF16)

    lane = lax.broadcasted_iota(jnp.int32, (1, LANES), 1)
    row = lax.broadcasted_iota(jnp.int32, (LANES, 1), 0)
    first_half = {A_DK: (lane % A_DK) < (A_DK // 2), HEAD_DIM: (lane % HEAD_DIM) < (HEAD_DIM // 2)}
    bd = {A_DK: bd32_ref, HEAD_DIM: bd64_ref}
    rope_idx = {A_DK: 0, HEAD_DIM: 2}
    count = {}

    n_ch = IN_W // MXU_W
    y_next = _dot(hb, w_ref[:, 0:MXU_W])
    for ch in range(n_ch):
        c0 = ch * MXU_W
        y = y_next
        if ch + 1 < n_ch:
            y_next = _dot(hb, w_ref[:, c0 + MXU_W:c0 + 2 * MXU_W])
        roles = _ROLES[2 * ch:2 * ch + 2]
        d = _NORM_D.get(roles[0]) or _NORM_D.get(roles[1])
        if d is not None:
            ss = _dot((y * y).astype(BF16), bd[d][...])
            yn = y * lax.rsqrt(ss * (1.0 / d) + EPS) * gain_ref[:, c0:c0 + MXU_W]
        for half in range(2):
            role = roles[half]
            j = count.get(role, 0)
            count[role] = j + 1
            sl = slice(half * LANES, (half + 1) * LANES)
            if role in _NORM_D:
                z = yn[:, sl]
                if rope and role in _ROPE_ROLES:
                    dd = _NORM_D[role]
                    zsw = jnp.where(first_half[dd], pltpu.roll(z, LANES - dd // 2, 1),
                                    pltpu.roll(z, dd // 2, 1))
                    z = z * rope_ref[rope_idx[dd]] + zsw * rope_ref[rope_idx[dd] + 1]
            else:
                z = y[:, sl]
            osl = slice(j * LANES, (j + 1) * LANES)
            if role == "qa":
                qa_ref[:, osl] = z.astype(BF16)
            elif role == "qb":
                qb_ref[:, osl] = z.astype(BF16)
            elif role == "qn":
                qc_ref[:, osl] = z.astype(BF16)
            elif role == "ka":
                zt = z.T
                for s in range(LANES // A_DK):
                    kta_ref[0, 4 * j + s] = jnp.where(row // A_DK == s, zt, 0.0).astype(BF16)
            elif role == "kb":
                zt = z.T
                zst = pltpu.roll(z, HEAD_DIM, 1).T
                lo = row < HEAD_DIM
                ktb_ref[0, 0] = jnp.where(lo, zt, 0.0).astype(BF16)
                ktb_ref[0, 1] = jnp.where(lo, 0.0, zst).astype(BF16)
                ktb_ref[0, 2] = jnp.where(lo, zst, 0.0).astype(BF16)
                ktb_ref[0, 3] = jnp.where(lo, 0.0, zt).astype(BF16)
            elif role == "kn":
                zt = z.T
                lo = row < HEAD_DIM
                ktc_ref[0, 2 * j] = jnp.where(lo, zt, 0.0).astype(BF16)
                ktc_ref[0, 2 * j + 1] = jnp.where(lo, 0.0, zt).astype(BF16)
            elif role in ("va", "vb", "vn"):
                v_ref = {"va": va_ref, "vb": vb_ref, "vn": vc_ref}[role]
                lo = lane < HEAD_DIM
                v_ref[0, 2 * j] = jnp.where(lo, z, 1.0).astype(BF16)
                v_ref[0, 2 * j + 1] = jnp.where(lo, pltpu.roll(z, HEAD_DIM, 1), 1.0).astype(BF16)
            elif role == "ga":
                ga_ref[:, osl] = _silu(z)
            elif role == "gb":
                gb_ref[:, osl] = _silu(z)
            elif role == "gn":
                gc_ref[:, osl] = _silu(z)


def _proj_call(xf, mod, ng, w, gain, rope_tab, bd32, bd64, *, nb, seq, tm, rope, mod_row, fuse=None):
    tpb = seq // tm
    t_all = nb * seq
    if mod_row is None:
        mod_map = lambda i: (i // tpb, 0, 0)
    else:
        mod_map = lambda i: (mod_row, 0, 0)
    tok = lambda w_: pl.BlockSpec((tm, w_), lambda i: (i, 0))
    const2 = lambda shp: pl.BlockSpec(shp, lambda i: (0, 0))
    in_specs = [
        tok(D_MODEL),
        pl.BlockSpec((1, 3, D_MODEL), mod_map),
        const2((1, D_MODEL)),
        const2((D_MODEL, IN_W)),
        const2((1, IN_W)),
        pl.BlockSpec((4, tm, LANES), lambda i: (0, i % tpb, 0)),
        const2((MXU_W, MXU_W)),
        const2((MXU_W, MXU_W)),
    ]
    out_shape = [
        jax.ShapeDtypeStruct((t_all, 256), BF16),
        jax.ShapeDtypeStruct((t_all, 384), BF16),
        jax.ShapeDtypeStruct((t_all, 384), BF16),
        jax.ShapeDtypeStruct((nb, 8, LANES, seq), BF16),
        jax.ShapeDtypeStruct((nb, 4, LANES, seq), BF16),
        jax.ShapeDtypeStruct((nb, 6, LANES, seq), BF16),
        jax.ShapeDtypeStruct((nb, 4, seq, LANES), BF16),
        jax.ShapeDtypeStruct((nb, 2, seq, LANES), BF16),
        jax.ShapeDtypeStruct((nb, 6, seq, LANES), BF16),
        jax.ShapeDtypeStruct((t_all, 256), F32),
        jax.ShapeDtypeStruct((t_all, 384), F32),
        jax.ShapeDtypeStruct((t_all, 384), F32),
    ]
    out_specs = [
        tok(256), tok(384), tok(384),
        pl.BlockSpec((1, 8, LANES, tm), lambda i: (i // tpb, 0, 0, i % tpb)),
        pl.BlockSpec((1, 4, LANES, tm), lambda i: (i // tpb, 0, 0, i % tpb)),
        pl.BlockSpec((1, 6, LANES, tm), lambda i: (i // tpb, 0, 0, i % tpb)),
        pl.BlockSpec((1, 4, tm, LANES), lambda i: (i // tpb, 0, i % tpb, 0)),
        pl.BlockSpec((1, 2, tm, LANES), lambda i: (i // tpb, 0, i % tpb, 0)),
        pl.BlockSpec((1, 6, tm, LANES), lambda i: (i // tpb, 0, i % tpb, 0)),
        tok(256), tok(384), tok(384),
    ]
    args = [xf, mod, ng, w, gain, rope_tab, bd32, bd64]
    if fuse is not None:
        ya, yb, yc, wo, modp = fuse
        in_specs += [tok(ya.shape[1]), tok(yb.shape[1]), tok(yc.shape[1]), const2(wo.shape),
                     pl.BlockSpec((1, 3, D_MODEL), mod_map)]
        out_shape.append(jax.ShapeDtypeStruct(xf.shape, F32))
        out_specs.append(tok(D_MODEL))
        args += [ya, yb, yc, wo, modp]
    return pl.pallas_call(
        functools.partial(_proj_kernel, rope=rope, tm=tm, fuse_out=fuse is not None),
        grid=(t_all // tm,),
        in_specs=in_specs,
        out_specs=out_specs,
        out_shape=out_shape,
        compiler_params=pltpu.CompilerParams(
            dimension_semantics=("arbitrary",), vmem_limit_bytes=VMEM_LIMIT),
        name="proj_lat" if rope else "proj_ctx",
    )(*args)


CHUNK = MXU_W
ATTN_CHUNK = 512
ATTN_TQ = 256
ATTN_N_SUB = 4
ATTN_A_N_SUB = 2


def _fold_lanes(x, op):
    acc = x[:, :LANES]
    for i in range(1, x.shape[1] // LANES):
        acc = op(acc, x[:, i * LANES:(i + 1) * LANES])
    return acc


def _run_units(n_units, chunks, score_chunk, value_chunk, s_scr, finish):
    def qk(u, c, m_run):
        k0, w = chunks[c]
        s = score_chunk(u, c)
        s_scr[u % 2, :, k0:k0 + w] = s
        part = _fold_lanes(s, jnp.maximum)
        return part if m_run is None else jnp.maximum(m_run, part)

    def sm(u, c, m, o):
        k0, w = chunks[c]
        p = jnp.exp2(s_scr[u % 2, :, k0:k0 + w] - m)
        d = _dot(p.astype(BF16), value_chunk(u, c))
        return d if o is None else o + d

    m_run = None
    for c in range(len(chunks)):
        m_run = qk(0, c, m_run)
    for u in range(n_units):
        m = jnp.max(m_run, axis=-1, keepdims=True)
        m_run, o = None, None
        for c in range(len(chunks)):
            if u + 1 < n_units:
                m_run = qk(u + 1, c, m_run)
            o = sm(u, c, m, o)
        finish(u, o)


def _normalise(o, half):
    r = pltpu.roll(o, HEAD_DIM, 1)
    return o * (1.0 / r) if half == 0 else r * (1.0 / o)


def _key_chunks(n_lat_keys, n_ctx_keys, width):
    chunks = [(k, width) for k in range(0, n_lat_keys, width)]
    return chunks + [(n_lat_keys, n_ctx_keys)]


def _diff_finisher(lam, lam_init, sub_ref, lane, emit):
    state = {}

    def finish(a, o):
        h, m = divmod(a, 2)
        grp, half = divmod(h, 2)
        o = _normalise(o, half)
        if m == 0:
            state["o0"] = o
            return
        o = jnp.where(lane // HEAD_DIM == half, state["o0"] - lam * o, 0.0)
        msq = jnp.sum(o * o, axis=-1, keepdims=True) * (1.0 / HEAD_DIM)
        yh = o * lax.rsqrt(msq + EPS)
        if half == 0:
            state["yg"] = yh
            return
        emit(grp, (state["yg"] + yh) * sub_ref[...] * (1.0 - lam_init))

    return finish


def _plain_finisher(lane, emit):
    state = {}

    def finish(grp, half, o):
        o = _normalise(o, half)
        if half == 0:
            state["yg"] = o
            return
        emit(grp, jnp.where(lane < HEAD_DIM, state["yg"], o))

    return finish


def _diff_lambda(lamp_ref, lam_init):
    lp = lamp_ref[...]
    l1 = jnp.sum(lp[0:1] * lp[1:2], axis=-1, keepdims=True)
    l2 = jnp.sum(lp[2:3] * lp[3:4], axis=-1, keepdims=True)
    return jnp.exp(l1) - jnp.exp(l2) + lam_init


def _attn_a_kernel(*refs, has_lat, lam_init, chunk_w, tq):
    if has_lat:
        lamp_ref, sub_ref, q_ref, g_ref, ktc_ref, vc_ref, kt_ref, v_ref, o_ref, s_scr = refs
    else:
        lamp_ref, sub_ref, q_ref, g_ref, ktc_ref, vc_ref, o_ref, s_scr = refs
    n_lat = kt_ref.shape[3] if has_lat else 0
    chunks = _key_chunks(n_lat, ktc_ref.shape[3], chunk_w)
    lam = _diff_lambda(lamp_ref, lam_init)
    lane = lax.broadcasted_iota(jnp.int32, (1, LANES), 1)
    n_sub = q_ref.shape[0] // tq

    def score_chunk(u, c):
        r, a = divmod(u, 2 * A_HEADS)
        grp = a // 4
        k0, w = chunks[c]
        ql = q_ref[r * tq:(r + 1) * tq, grp * LANES:(grp + 1) * LANES]
        if k0 < n_lat:
            return _dot(ql, kt_ref[0, a, :, k0:k0 + w])
        return _dot(ql, ktc_ref[0, a])

    def value_chunk(u, c):
        a = u % (2 * A_HEADS)
        k0, w = chunks[c]
        if k0 < n_lat:
            return v_ref[0, a // 2, k0:k0 + w, :]
        return vc_ref[0, a // 2]

    def make_fin(r):
        rows = slice(r * tq, (r + 1) * tq)

        def emit(grp, y):
            sl = slice(grp * LANES, (grp + 1) * LANES)
            o_ref[rows, sl] = (y * g_ref[rows, sl]).astype(BF16)

        return _diff_finisher(lam, lam_init, sub_ref, lane, emit)

    fin = [make_fin(r) for r in range(n_sub)]

    def finish(u, o):
        r, a = divmod(u, 2 * A_HEADS)
        fin[r](a, o)

    _run_units(n_sub * 2 * A_HEADS, chunks, score_chunk, value_chunk, s_scr, finish)


def _attn_plain_kernel(*refs, has_lat, heads, chunk_w, tq):
    if has_lat:
        q_ref, g_ref, ktc_ref, vc_ref, kt_ref, v_ref, o_ref, s_scr = refs
    else:
        q_ref, g_ref, ktc_ref, vc_ref, o_ref, s_scr = refs
    n_lat = kt_ref.shape[3] if has_lat else 0
    chunks = _key_chunks(n_lat, ktc_ref.shape[-1], chunk_w)
    lane = lax.broadcasted_iota(jnp.int32, (1, LANES), 1)
    n_sub = q_ref.shape[0] // tq
    units = [(r, grp, half, ki, vi) for r in range(n_sub)
             for grp, pair in enumerate(heads) for half, (ki, vi) in enumerate(pair)]

    def score_chunk(u, c):
        r, grp, _, ki, _ = units[u]
        k0, w = chunks[c]
        ql = q_ref[r * tq:(r + 1) * tq, grp * LANES:(grp + 1) * LANES]
        if k0 < n_lat:
            return _dot(ql, kt_ref[0, ki, :, k0:k0 + w])
        return _dot(ql, ktc_ref[0, ki])

    def value_chunk(u, c):
        vi = units[u][4]
        k0, w = chunks[c]
        if k0 < n_lat:
            return v_ref[0, vi, k0:k0 + w, :]
        return vc_ref[0, vi]

    def make_fin(r):
        rows = slice(r * tq, (r + 1) * tq)

        def emit(grp, y):
            sl = slice(grp * LANES, (grp + 1) * LANES)
            o_ref[rows, sl] = (y * g_ref[rows, sl]).astype(BF16)

        return _plain_finisher(lane, emit)

    fin = [make_fin(r) for r in range(n_sub)]

    def finish(u, o):
        r, grp, half = units[u][:3]
        fin[r](grp, half, o)

    _run_units(len(units), chunks, score_chunk, value_chunk, s_scr, finish)


def _attn_c_kernel(q_ref, g_ref, ktc_ref, vc_ref, kt_ref, v_ref, bias_ref, o_ref, s_scr, *, nwin, n_sub):
    n_tiles = pl.num_programs(1) * n_sub
    lane = lax.broadcasted_iota(jnp.int32, (1, LANES), 1)
    win = nwin * CHUNK
    chunks = [(0, win), (win, ktc_ref.shape[3])]
    tile = [pl.program_id(1) * n_sub + r for r in range(n_sub)]
    j0 = [jnp.clip(t - 1, 0, kt_ref.shape[3] // CHUNK - nwin) for t in tile]
    k0 = [pl.multiple_of(j * CHUNK, CHUNK) for j in j0]
    kind = [jnp.where(t == 0, 0, jnp.where(t == n_tiles - 1, 2, 1)) for t in tile]

    def score_chunk(u, c):
        r, h = divmod(u, C_HEADS)
        grp = h // 2
        ql = q_ref[r * _C_TQ:(r + 1) * _C_TQ, grp * LANES:(grp + 1) * LANES]
        if c == 0:
            return _dot(ql, kt_ref[0, h, :, pl.ds(k0[r], win)]) + bias_ref[kind[r], h]
        return _dot(ql, ktc_ref[0, h])

    def value_chunk(u, c):
        r, h = divmod(u, C_HEADS)
        if c == 0:
            return v_ref[0, h, pl.ds(k0[r], win), :]
        return vc_ref[0, h]

    def make_fin(r):
        rows = slice(r * _C_TQ, (r + 1) * _C_TQ)

        def emit(grp, y):
            sl = slice(grp * LANES, (grp + 1) * LANES)
            o_ref[rows, sl] = (y * g_ref[rows, sl]).astype(BF16)

        return _plain_finisher(lane, emit)

    fin = [make_fin(r) for r in range(n_sub)]

    def finish(u, o):
        r, h = divmod(u, C_HEADS)
        fin[r](h // 2, h % 2, o)

    _run_units(n_sub * C_HEADS, chunks, score_chunk, value_chunk, s_scr, finish)


def _attn_specs(width, tq, nq_tiles, kt_shape, v_shape, ktc_shape, vc_shape, has_lat):
    tokq = pl.BlockSpec((tq, width), lambda b, t: (b * nq_tiles + t, 0))

    def per_batch(shape):
        nd = len(shape)
        return pl.BlockSpec((1,) + tuple(shape[1:]), lambda b, t: (b,) + (0,) * (nd - 1))

    specs = [tokq, tokq, per_batch(ktc_shape), per_batch(vc_shape)]
    if has_lat:
        specs += [per_batch(kt_shape), per_batch(v_shape)]
    return specs, tokq


def _score_scratch(tq, n_keys):
    return [pltpu.VMEM((2, tq, n_keys), F32)]


def _attn_a_call(lamp, sub, q, g, ktc, vc, kt, v, *, nb, sq, tq, lam_init, n_sub=1):
    has_lat = kt is not None
    nqt = sq // (tq * n_sub)
    n_keys = ktc.shape[-1] + (kt.shape[-1] if has_lat else 0)
    specs, tokq = _attn_specs(256, tq * n_sub, nqt, kt.shape if has_lat else None, v.shape if has_lat else None,
                              ktc.shape, vc.shape, has_lat)
    small = [pl.BlockSpec((8, LANES), lambda b, t: (0, 0)), pl.BlockSpec((1, LANES), lambda b, t: (0, 0))]
    args = [lamp, sub, q, g, ktc, vc] + ([kt, v] if has_lat else [])
    return pl.pallas_call(
        functools.partial(_attn_a_kernel, has_lat=has_lat, lam_init=lam_init, chunk_w=ATTN_CHUNK, tq=tq),
        grid=(nb, nqt),
        in_specs=small + specs,
        out_specs=tokq,
        out_shape=jax.ShapeDtypeStruct((nb * sq, 256), BF16),
        scratch_shapes=_score_scratch(tq, n_keys),
        compiler_params=pltpu.CompilerParams(
            dimension_semantics=("arbitrary", "arbitrary"), vmem_limit_bytes=VMEM_LIMIT),
        name="attn_a_lat" if has_lat else "attn_a_ctx",
    )(*args)


def _attn_plain_call(q, g, ktc, vc, kt, v, *, nb, sq, tq, heads, name, n_sub=1):
    has_lat = kt is not None
    nqt = sq // (tq * n_sub)
    n_keys = ktc.shape[-1] + (kt.shape[-1] if has_lat else 0)
    specs, tokq = _attn_specs(384, tq * n_sub, nqt, kt.shape if has_lat else None, v.shape if has_lat else None,
                              ktc.shape, vc.shape, has_lat)
    args = [q, g, ktc, vc] + ([kt, v] if has_lat else [])
    return pl.pallas_call(
        functools.partial(_attn_plain_kernel, has_lat=has_lat, heads=heads, chunk_w=ATTN_CHUNK, tq=tq),
        grid=(nb, nqt),
        in_specs=specs,
        out_specs=tokq,
        out_shape=jax.ShapeDtypeStruct((nb * sq, 384), BF16),
        scratch_shapes=_score_scratch(tq, n_keys),
        compiler_params=pltpu.CompilerParams(
            dimension_semantics=("arbitrary", "arbitrary"), vmem_limit_bytes=VMEM_LIMIT),
        name=name,
    )(*args)


_C_TQ = 4 * GRID_W
_C_NWIN = 3


def _attn_c_call(q, g, ktc, vc, kt, v, bias, *, nb, sq, n_sub):
    nqt = sq // (_C_TQ * n_sub)
    specs, tokq = _attn_specs(384, _C_TQ * n_sub, nqt, kt.shape, v.shape, ktc.shape, vc.shape, True)
    specs.append(pl.BlockSpec(bias.shape, lambda b, t: (0, 0, 0, 0), pipeline_mode=pl.Buffered(1)))
    return pl.pallas_call(
        functools.partial(_attn_c_kernel, nwin=_C_NWIN, n_sub=n_sub),
        grid=(nb, nqt),
        in_specs=specs,
        out_specs=tokq,
        out_shape=jax.ShapeDtypeStruct((nb * sq, 384), BF16),
        scratch_shapes=_score_scratch(_C_TQ, (_C_NWIN + 1) * CHUNK),
        compiler_params=pltpu.CompilerParams(
            dimension_semantics=("arbitrary", "arbitrary"), vmem_limit_bytes=VMEM_LIMIT),
        name="attn_c_lat",
    )(q, g, ktc, vc, kt, v, bias)


def _ctx_kernel(x_ref, mod_ref, lamp_ref, sub_ref, qa_ref, qb_ref, qc_ref, ga_ref, gb_ref, gc_ref,
                kta_ref, ktb_ref, ktc_ref, va_ref, vb_ref, vc_ref, w_ref, o_ref, s_scr, y_scr, *, lam_init):
    lam = _diff_lambda(lamp_ref, lam_init)
    lane = lax.broadcasted_iota(jnp.int32, (1, LANES), 1)
    chunks = [(0, kta_ref.shape[3])]
    b_units = [(grp, half, ki, vi) for grp, pair in enumerate(_B_HEADS_MAP) for half, (ki, vi) in enumerate(pair)]
    n_a, n_b = 2 * A_HEADS, len(b_units)

    def score_chunk(u, c):
        if u < n_a:
            return _dot(qa_ref[:, (u // 4) * LANES:(u // 4 + 1) * LANES], kta_ref[0, u])
        if u < n_a + n_b:
            grp, _, ki, _ = b_units[u - n_a]
            return _dot(qb_ref[:, grp * LANES:(grp + 1) * LANES], ktb_ref[0, ki])
        h = u - n_a - n_b
        return _dot(qc_ref[:, (h // 2) * LANES:(h // 2 + 1) * LANES], ktc_ref[0, h])

    def value_chunk(u, c):
        if u < n_a:
            return va_ref[0, u // 2]
        if u < n_a + n_b:
            return vb_ref[0, b_units[u - n_a][3]]
        return vc_ref[0, u - n_a - n_b]

    def emitter(g_ref, col0):
        def emit(grp, y):
            sl = slice(grp * LANES, (grp + 1) * LANES)
            y_scr[:, col0 + grp * LANES:col0 + (grp + 1) * LANES] = (y * g_ref[:, sl]).astype(BF16)
        return emit

    fin_a = _diff_finisher(lam, lam_init, sub_ref, lane, emitter(ga_ref, 0))
    fin_b = _plain_finisher(lane, emitter(gb_ref, qa_ref.shape[1]))
    fin_c = _plain_finisher(lane, emitter(gc_ref, qa_ref.shape[1] + qb_ref.shape[1]))

    def finish(u, o):
        if u < n_a:
            fin_a(u, o)
        elif u < n_a + n_b:
            grp, half = b_units[u - n_a][:2]
            fin_b(grp, half, o)
        else:
            h = u - n_a - n_b
            fin_c(h // 2, h % 2, o)

    _run_units(n_a + n_b + C_HEADS, chunks, score_chunk, value_chunk, s_scr, finish)
    o_ref[...] = x_ref[...] + mod_ref[0, 2:3, :] * _dot(y_scr[...], w_ref[...])


def _ctx_call(cf, mod, lamp, sub, cx, w, *, nb, n_ctx, mod_row, lam_init):
    qa, qb, qc, kta, ktb, ktc, va, vb, vc, ga, gb, gc = cx
    tok = lambda w_: pl.BlockSpec((n_ctx, w_), lambda b: (b, 0))
    per_b = lambda a: pl.BlockSpec((1,) + tuple(a.shape[1:]), lambda b: (b, 0, 0, 0))
    const = lambda shp: pl.BlockSpec(shp, lambda b: (0,) * len(shp))
    in_specs = [tok(D_MODEL), pl.BlockSpec((1, 3, D_MODEL), lambda b: (mod_row, 0, 0)),
                const((8, LANES)), const((1, LANES)),
                tok(qa.shape[1]), tok(qb.shape[1]), tok(qc.shape[1]),
                tok(ga.shape[1]), tok(gb.shape[1]), tok(gc.shape[1]),
                per_b(kta), per_b(ktb), per_b(ktc), per_b(va), per_b(vb), per_b(vc),
                const((D_MODEL, D_MODEL))]
    return pl.pallas_call(
        functools.partial(_ctx_kernel, lam_init=lam_init),
        grid=(nb,),
        in_specs=in_specs,
        out_specs=tok(D_MODEL),
        out_shape=jax.ShapeDtypeStruct(cf.shape, F32),
        scratch_shapes=[pltpu.VMEM((2, n_ctx, n_ctx), F32), pltpu.VMEM((n_ctx, D_MODEL), BF16)],
        compiler_params=pltpu.CompilerParams(
            dimension_semantics=("arbitrary",), vmem_limit_bytes=VMEM_LIMIT),
        name="ctx_block",
    )(cf, mod, lamp, sub, qa, qb, qc, ga, gb, gc, kta, ktb, ktc, va, vb, vc, w)


_RPB_H = 2 * WIN_H - 1
_RPB_W = 2 * WIN_W - 1


def _bias_kernel(rpb_ref, o_ref, *, rows):
    base = pl.program_id(0) * (_RPB_H * _RPB_W)
    lane = lax.broadcasted_iota(jnp.int32, (GRID_W, LANES), 1)
    qc = lax.broadcasted_iota(jnp.int32, (GRID_W, LANES), 0)
    kc = lane % GRID_W
    dcol = kc - qc + (WIN_W - 1)
    c0 = jnp.clip(qc - WIN_W // 2, 0, GRID_W - WIN_W)
    ok_c = (kc >= c0) & (kc < c0 + WIN_W)
    lo = lane < GRID_W
    kh = min(WIN_H, rows)
    nqt = rows // 4
    tables = []
    for dr in range(_RPB_H):
        acc = jnp.full((GRID_W, LANES), rpb_ref[base + dr * _RPB_W], F32)
        for dc in range(1, _RPB_W):
            acc = jnp.where(dcol >= dc, rpb_ref[base + dr * _RPB_W + dc], acc)
        tables.append(acc * LOG2E)
    for kind, t in enumerate((0, 1, nqt - 1)):
        ks = min(max(4 * t - 4, 0), rows - 4 * _C_NWIN)
        for i in range(4):
            qr = 4 * t + i
            r0 = min(max(qr - kh // 2, 0), rows - kh)
            for jj in range(_C_NWIN * 2):
                kra = ks + 2 * jj
                ok_a = r0 <= kra < r0 + kh
                ok_b = r0 <= kra + 1 < r0 + kh
                dra = kra - qr + (WIN_H - 1)
                if ok_a and ok_b:
                    piece = jnp.where(ok_c, jnp.where(lo, tables[dra], tables[dra + 1]), NEG)
                elif ok_a:
                    piece = jnp.where(ok_c & lo, tables[dra], NEG)
                elif ok_b:
                    piece = jnp.where(ok_c & jnp.logical_not(lo), tables[dra + 1], NEG)
                else:
                    piece = jnp.full((GRID_W, LANES), NEG, F32)
                o_ref[kind, 0, i * GRID_W:(i + 1) * GRID_W, jj * LANES:(jj + 1) * LANES] = piece


def _bias_call(rpb_flat, rows):
    return pl.pallas_call(
        functools.partial(_bias_kernel, rows=rows),
        grid=(C_HEADS,),
        in_specs=[pl.BlockSpec(memory_space=pltpu.SMEM)],
        out_specs=pl.BlockSpec((3, 1, _C_TQ, _C_NWIN * MXU_W), lambda h: (0, h, 0, 0)),
        out_shape=jax.ShapeDtypeStruct((3, C_HEADS, _C_TQ, _C_NWIN * MXU_W), F32),
        compiler_params=pltpu.CompilerParams(
            dimension_semantics=("arbitrary",), vmem_limit_bytes=VMEM_LIMIT),
        name="nat_bias",
    )(rpb_flat)


def _out_kernel(x_ref, mod_ref, ya_ref, yb_ref, yc_ref, w_ref, o_ref):
    y = jnp.concatenate([ya_ref[...], yb_ref[...], yc_ref[...]], axis=1)
    o_ref[...] = x_ref[...] + mod_ref[0, 2:3, :] * _dot(y, w_ref[...])


def _out_call(xf, mod, ya, yb, yc, w, *, seq, tm, mod_row, name):
    tpb = seq // tm
    if mod_row is None:
        mod_map = lambda i: (i // tpb, 0, 0)
    else:
        mod_map = lambda i: (mod_row, 0, 0)
    tok = lambda w_: pl.BlockSpec((tm, w_), lambda i: (i, 0))
    return pl.pallas_call(
        _out_kernel,
        grid=(xf.shape[0] // tm,),
        in_specs=[tok(D_MODEL), pl.BlockSpec((1, 3, D_MODEL), mod_map), tok(256), tok(384), tok(384),
                  pl.BlockSpec((D_MODEL, D_MODEL), lambda i: (0, 0))],
        out_specs=tok(D_MODEL),
        out_shape=jax.ShapeDtypeStruct(xf.shape, F32),
        compiler_params=pltpu.CompilerParams(
            dimension_semantics=("arbitrary",), vmem_limit_bytes=VMEM_LIMIT),
        name=name,
    )(xf, mod, ya, yb, yc, w)


def _rope_tables(seq):
    t = np.arange(seq)
    row = (t // GRID_W).astype(np.float32)
    col = (t % GRID_W).astype(np.float32)
    out = []
    for d in (A_DK, HEAD_DIM):
        nf = d // 4
        inv = (np.float32(ROPE_THETA) ** (-np.arange(nf, dtype=np.float32) / np.float32(nf))).astype(np.float32)
        ang = np.concatenate([row[:, None] * inv, col[:, None] * inv], axis=-1).astype(np.float32)
        cos = np.cos(ang).astype(np.float32)
        sin = np.sin(ang).astype(np.float32)
        cos_h = np.concatenate([cos, cos], axis=-1)
        sin_h = np.concatenate([-sin, sin], axis=-1)
        out.append(np.tile(cos_h, (1, LANES // d)))
        out.append(np.tile(sin_h, (1, LANES // d)))
    return np.stack(out).astype(np.float32)


def _block_diag(d):
    i = np.arange(MXU_W)
    return (i[:, None] // d == i[None, :] // d).astype(np.float32)


_B_HEADS_MAP = tuple(tuple(((h // 3) * 2 + (h % 2), h // 3) for h in (2 * g, 2 * g + 1)) for g in range(3))
_C_HEADS_MAP = tuple(tuple((h, h) for h in (2 * g, 2 * g + 1)) for g in range(3))


def kernel(x, c, ctx, c_ctx, norm_g, w_ada, b_ada, w_in, w_out, diff_q_norm, diff_k_norm, lambda_q1, lambda_k1,
           lambda_q2, lambda_k2, diff_subln, gqa_q_norm, gqa_k_norm, nat_q_norm, nat_k_norm, nat_rpb):
    nb, seq, _ = x.shape
    n_ctx = ctx.shape[1]
    rows = seq // GRID_W

    cc = jnp.concatenate([c, c_ctx[None, :], jnp.zeros((16 - nb - 1, D_MODEL), F32)], axis=0)
    mod_all = _ada_call(cc, w_ada, b_ada.reshape(DEPTH, 1, 3 * D_MODEL)).reshape(DEPTH, 16, 3, D_MODEL)

    rope_lat = jnp.asarray(_rope_tables(seq))
    rope_ctx = jnp.zeros((4, n_ctx, LANES), F32)
    bd32 = jnp.asarray(_block_diag(A_DK), BF16)
    bd64 = jnp.asarray(_block_diag(HEAD_DIM), BF16)

    xf = x.reshape(nb * seq, D_MODEL)
    cf = ctx.reshape(nb * n_ctx, D_MODEL)
    ones128 = jnp.ones((LANES,), F32)
    pending = None

    for l in range(DEPTH):
        lam_init = 0.8 - 0.6 * math.exp(-0.3 * l)
        last = l == DEPTH - 1
        w_l = w_in[l].astype(BF16)
        wo_l = w_out[l].astype(BF16)
        mod = mod_all[l]
        ng = norm_g[l].reshape(1, D_MODEL)
        sa = A_DK ** -0.5 * LOG2E
        sb = HEAD_DIM ** -0.5 * LOG2E
        gain = jnp.concatenate([
            jnp.tile(diff_q_norm[l], 8) * sa, jnp.tile(diff_k_norm[l], 8), ones128, ones128, ones128, ones128,
            jnp.tile(gqa_q_norm[l], 6) * sb, jnp.tile(gqa_k_norm[l], 2), ones128, ones128, ones128, ones128,
            jnp.tile(nat_q_norm[l], 6) * sb, jnp.tile(nat_k_norm[l], 6),
            ones128, ones128, ones128, ones128, ones128, ones128]).reshape(1, IN_W)
        lamp = jnp.zeros((8, LANES), F32)
        lamp = lamp.at[0, :A_DK].set(lambda_q1[l]).at[1, :A_DK].set(lambda_k1[l])
        lamp = lamp.at[2, :A_DK].set(lambda_q2[l]).at[3, :A_DK].set(lambda_k2[l])
        sub = jnp.tile(diff_subln[l], 2).reshape(1, LANES)
        bias = _bias_call(nat_rpb[l].reshape(-1), rows)

        lat = _proj_call(xf, mod, ng, w_l, gain, rope_lat, bd32, bd64,
                         nb=nb, seq=seq, tm=512, rope=True, mod_row=None, fuse=pending)
        if pending is not None:
            xf = lat[12]
        cx = _proj_call(cf, mod, ng, w_l, gain, rope_ctx, bd32, bd64,
                        nb=nb, seq=n_ctx, tm=n_ctx, rope=False, mod_row=nb)
        qa, qb, qc, kta, ktb, ktc, va, vb, vc, ga, gb, gc = lat[:12]
        qa_c, qb_c, qc_c, kta_c, ktb_c, ktc_c, va_c, vb_c, vc_c, ga_c, gb_c, gc_c = cx

        ya = _attn_a_call(lamp, sub, qa, ga, kta_c, va_c, kta, va, nb=nb, sq=seq, tq=ATTN_TQ, lam_init=lam_init,
                          n_sub=ATTN_A_N_SUB)
        yb = _attn_plain_call(qb, gb, ktb_c, vb_c, ktb, vb, nb=nb, sq=seq, tq=ATTN_TQ, heads=_B_HEADS_MAP,
                              name="attn_b_lat", n_sub=ATTN_N_SUB)
        yc = _attn_c_call(qc, gc, ktc_c, vc_c, ktc, vc, bias, nb=nb, sq=seq, n_sub=ATTN_N_SUB)
        if last:
            xf = _out_call(xf, mod, ya, yb, yc, wo_l, seq=seq, tm=512, mod_row=None, name="out_lat")
        else:
            pending = (ya, yb, yc, wo_l, mod)

        if not last:
            cf = _ctx_call(cf, mod, lamp, sub, cx, wo_l, nb=nb, n_ctx=n_ctx, mod_row=nb, lam_init=lam_init)

    return xf.reshape(nb, seq, D_MODEL)
```
